```python
import jax
import jax.numpy as jnp
from jax import lax
import numpy as np

D_MODEL = 1024
BATCH = 2
SEQ = 16384
DEPTH = 2

HEAD_DIM = 64
RWKV_HEADS = 6
FOX_HEADS = 6
POOL_GROUPS = 4
POOL_GROUP_DIM = 64
POOL_WINDOWS = (2, 4, 8, 16)
RWKV_WIDTH = RWKV_HEADS * HEAD_DIM
FOX_WIDTH = FOX_HEADS * HEAD_DIM
POOL_WIDTH = POOL_GROUPS * POOL_GROUP_DIM
D_MIX = RWKV_WIDTH + FOX_WIDTH + POOL_WIDTH
DECAY_LORA = 64
ICLR_LORA = 64
VRES_LORA = 32
GATE_LORA = 160
FFN_HIDDEN = -(-8 * D_MODEL // (3 * 256)) * 256
Q_BLOCK = 128
RMS_EPS = 1e-6
GN_EPS = HEAD_DIM * 1e-5

RWKV_COLS = (RWKV_WIDTH, RWKV_WIDTH, RWKV_WIDTH, DECAY_LORA, ICLR_LORA, GATE_LORA)
RWKV_IN = sum(RWKV_COLS)
IN_COLS = (RWKV_IN, FOX_WIDTH, FOX_WIDTH, FOX_WIDTH, FOX_WIDTH, FOX_HEADS, POOL_WIDTH)
D_IN = sum(IN_COLS)

kernel_name = 'hybrid_rwkv7_fox_pool_block'


def _split_cols(t, sizes):
    idx, acc = [], 0
    for s in sizes[:-1]:
        acc += s
        idx.append(acc)
    return jnp.split(t, idx, axis=-1)


def _rms_f32(x, g):
    xf = x.astype(jnp.float32)
    y = xf * lax.rsqrt(jnp.mean(xf * xf, axis=-1, keepdims=True) + RMS_EPS)
    return y * g.astype(jnp.float32)


def rms_norm(x, g):
    return _rms_f32(x, g).astype(x.dtype)


def token_shift(t):
    return jnp.pad(t, ((0, 0), (1, 0), (0, 0)))[:, :-1]


def wkv7_scan(r, decay, k, v, kk, a):
    B, S, H, N = r.shape
    xs = tuple(jnp.moveaxis(t, 1, 0) for t in (r, decay, k, v, kk, a))

    def step(state, inp):
        r_t, w_t, k_t, v_t, kk_t, a_t = inp
        sa = jnp.einsum('bhvk,bhk->bhv', state, -kk_t)
        state = (state * w_t[:, :, None, :]
                 + sa[..., None] * (kk_t * a_t)[:, :, None, :]
                 + v_t[..., None] * k_t[:, :, None, :])
        return state, jnp.einsum('bhvk,bhk->bhv', state, r_t)

    init = jnp.zeros((B, H, N, N), jnp.float32)
    _, ys = lax.scan(step, init, xs)
    return jnp.moveaxis(ys, 0, 1)


def rwkv7_time_mix(feats, v_first, mu, w0, w2, a0, a2, g2, k_k, k_a, r_k, ln_w, ln_b, v_res):
    dtype = feats.dtype
    B, S, _ = feats.shape
    f = feats.astype(jnp.float32)
    f = f + (token_shift(f) - f) * mu.astype(jnp.float32)
    r, k, v, wd, ad, gd = _split_cols(f, RWKV_COLS)
    w_log = -jax.nn.softplus(-(w0 + jnp.tanh(wd) @ w2)) - 0.5
    decay = jnp.exp(-jnp.exp(w_log.astype(jnp.float32)))
    a = jax.nn.sigmoid(a0 + ad @ a2)
    g = jax.nn.sigmoid(gd) @ g2
    if v_res is None:
        v_first = v
    else:
        v0, v1, v2 = v_res
        v = v + (v_first - v) * jax.nn.sigmoid(v0 + (v @ v1) @ v2)
    heads = lambda t: t.astype(jnp.float32).reshape(B, S, RWKV_HEADS, HEAD_DIM)
    kk = heads(k * k_k)
    kk = kk / jnp.maximum(jnp.sqrt(jnp.sum(kk * kk, axis=-1, keepdims=True)), 1e-12)
    k = k * (1.0 + (a - 1.0) * k_a)
    r_h, k_h, v_h, w_h, a_h = heads(r), heads(k), heads(v), heads(decay), heads(a)
    y = wkv7_scan(r_h, w_h, k_h, v_h, kk, a_h)
    mean = jnp.mean(y, axis=-1, keepdims=True)
    var = jnp.mean(jnp.square(y - mean), axis=-1, keepdims=True)
    y = ((y - mean) * lax.rsqrt(var + GN_EPS)).reshape(B, S, RWKV_WIDTH) * ln_w + ln_b
    bonus = jnp.sum(r_h * k_h * r_k.astype(jnp.float32), axis=-1, keepdims=True) * v_h
    y = (y + bonus.reshape(B, S, RWKV_WIDTH)) * g
    return y.astype(dtype), v_first


def forgetting_attention(q, k, v, gate, f_logit, q_g, k_g, f_b):
    dtype = q.dtype
    B, S, _ = q.shape
    nb = S // Q_BLOCK
    split = lambda t: t.reshape(B, S, FOX_HEADS, HEAD_DIM)
    qh = _rms_f32(split(q), q_g)
    kh = _rms_f32(split(k), k_g)
    vh = split(v).astype(jnp.float32)
    log_f = jax.nn.log_sigmoid(f_logit.astype(jnp.float32) + f_b.astype(jnp.float32))
    cum = jnp.cumsum(log_f, axis=1)

    def blocks(t):
        return t.reshape(B, nb, Q_BLOCK, FOX_HEADS, HEAD_DIM).transpose(1, 0, 3, 2, 4)

    qb, kb, vb = blocks(qh), blocks(kh), blocks(vh)
    cb = cum.reshape(B, nb, Q_BLOCK, FOX_HEADS).transpose(1, 0, 3, 2)
    offs = jnp.arange(Q_BLOCK)
    scale = HEAD_DIM ** -0.5

    def one_query_block(args):
        q_i, c_i, i = args
        q_pos = i * Q_BLOCK + offs

        def kv_step(j, carry):
            m, l, acc = carry
            k_j, v_j, c_j = kb[j], vb[j], cb[j]
            k_pos = j * Q_BLOCK + offs
            s = (jnp.einsum('bhqd,bhkd->bhqk', q_i, k_j) * scale
                 + c_i[..., :, None] - c_j[..., None, :])
            s = jnp.where(k_pos[None, :] <= q_pos[:, None], s, -jnp.inf)
            m_new = jnp.maximum(m, jnp.max(s, axis=-1))
            p = jnp.exp(s - m_new[..., None])
            alpha = jnp.exp(m - m_new)
            return (m_new, l * alpha + jnp.sum(p, axis=-1),
                    acc * alpha[..., None] + jnp.einsum('bhqk,bhkd->bhqd', p, v_j))

        init = (jnp.full((B, FOX_HEADS, Q_BLOCK), -jnp.inf, jnp.float32),
                jnp.zeros((B, FOX_HEADS, Q_BLOCK), jnp.float32),
                jnp.zeros((B, FOX_HEADS, Q_BLOCK, HEAD_DIM), jnp.float32))
        _, l, acc = lax.fori_loop(jnp.int32(0), i + 1, kv_step, init)
        return acc / l[..., None]

    out = lax.map(one_query_block, (qb, cb, jnp.arange(nb, dtype=jnp.int32)))
    out = out.transpose(1, 0, 3, 2, 4).reshape(B, S, FOX_WIDTH)
    return (out * jax.nn.sigmoid(gate.astype(jnp.float32))).astype(dtype)


def multiscale_pool(p, w, scale):
    dtype = p.dtype
    B, S, _ = p.shape
    pf = p.astype(jnp.float32).reshape(B, S, POOL_GROUPS, POOL_GROUP_DIM)
    cs = jnp.cumsum(pf, axis=1)
    count_base = jnp.arange(1, S + 1, dtype=jnp.float32)
    outs = []
    for gi, win in enumerate(POOL_WINDOWS):
        c = cs[:, :, gi]
        lag = jnp.pad(c, ((0, 0), (win, 0), (0, 0)))[:, :S]
        mean = (c - lag) / jnp.minimum(count_base, float(win))[None, :, None]
        outs.append(mean - pf[:, :, gi])
    u = jnp.stack(outs, axis=2)
    y = jnp.einsum('bsgc,gcd->bsgd', u, w.astype(jnp.float32)).reshape(B, S, POOL_WIDTH)
    return (y * scale.astype(jnp.float32)).astype(dtype)


def setup_inputs(seed: int = 0) -> dict:
    key = jax.random.key(seed)
    ks = iter(jax.random.split(key, 32))
    L = DEPTH
    f32 = jnp.float32

    def nrm(shape, s):
        return s * jax.random.normal(next(ks), shape, f32)

    def unif(shape, lo, hi):
        return jax.random.uniform(next(ks), shape, f32, lo, hi)

    return {
        'x': nrm((BATCH, SEQ, D_MODEL), 1.0),
        'mix_pre_g': 1.0 + nrm((L, D_MODEL), 0.05),
        'mix_post_g': 1.0 + nrm((L, D_MODEL), 0.05),
        'ffn_pre_g': 1.0 + nrm((L, D_MODEL), 0.05),
        'ffn_post_g': 1.0 + nrm((L, D_MODEL), 0.05),
        'w_in': nrm((L, D_MODEL, D_IN), D_MODEL ** -0.5),
        'w_out': nrm((L, D_MIX, D_MODEL), D_MIX ** -0.5),
        'rwkv_mu': unif((L, RWKV_IN), 0.0, 1.0),
        'rwkv_w0': unif((L, RWKV_WIDTH), -5.0, 1.0),
        'rwkv_w2': nrm((L, DECAY_LORA, RWKV_WIDTH), 0.5 * DECAY_LORA ** -0.5),
        'rwkv_a0': nrm((L, RWKV_WIDTH), 0.1),
        'rwkv_a2': nrm((L, ICLR_LORA, RWKV_WIDTH), 0.5 * ICLR_LORA ** -0.5),
        'rwkv_g2': nrm((L, GATE_LORA, RWKV_WIDTH), GATE_LORA ** -0.5),
        'rwkv_v0': nrm((L - 1, RWKV_WIDTH), 0.1),
        'rwkv_v1': nrm((L - 1, RWKV_WIDTH, VRES_LORA), RWKV_WIDTH ** -0.5),
        'rwkv_v2': nrm((L - 1, VRES_LORA, RWKV_WIDTH), 0.5 * VRES_LORA ** -0.5),
        'rwkv_k_k': 0.85 + nrm((L, RWKV_WIDTH), 0.05),
        'rwkv_k_a': 1.0 + nrm((L, RWKV_WIDTH), 0.05),
        'rwkv_r_k': nrm((L, RWKV_HEADS, HEAD_DIM), 0.1),
        'rwkv_ln_w': 1.0 + nrm((L, RWKV_WIDTH), 0.05),
        'rwkv_ln_b': nrm((L, RWKV_WIDTH), 0.02),
        'fox_q_g': 1.0 + nrm((L, HEAD_DIM), 0.05),
        'fox_k_g': 1.0 + nrm((L, HEAD_DIM), 0.05),
        'fox_f_b': unif((L, FOX_HEADS), 1.0, 4.0),
        'pool_w': nrm((L, POOL_GROUPS, POOL_GROUP_DIM, POOL_GROUP_DIM), POOL_GROUP_DIM ** -0.5),
        'pool_scale': 1.0 + nrm((L, POOL_WIDTH), 0.1),
        'ffn_w_gu': nrm((L, D_MODEL, 2 * FFN_HIDDEN), D_MODEL ** -0.5),
        'ffn_w_down': nrm((L, FFN_HIDDEN, D_MODEL), FFN_HIDDEN ** -0.5),
    }


def reference(x, mix_pre_g, mix_post_g, ffn_pre_g, ffn_post_g, w_in, w_out,
              rwkv_mu, rwkv_w0, rwkv_w2, rwkv_a0, rwkv_a2, rwkv_g2, rwkv_v0, rwkv_v1, rwkv_v2,
              rwkv_k_k, rwkv_k_a, rwkv_r_k, rwkv_ln_w, rwkv_ln_b,
              fox_q_g, fox_k_g, fox_f_b, pool_w, pool_scale, ffn_w_gu, ffn_w_down):
    v_first = None
    for l in range(DEPTH):
        h = rms_norm(x, mix_pre_g[l])
        rwkv_in, fq, fk, fv, fg, ff, pool_in = _split_cols(h @ w_in[l], IN_COLS)
        v_res = None if l == 0 else (rwkv_v0[l - 1], rwkv_v1[l - 1], rwkv_v2[l - 1])
        y_rwkv, v_first = rwkv7_time_mix(rwkv_in, v_first, rwkv_mu[l], rwkv_w0[l], rwkv_w2[l],
                                         rwkv_a0[l], rwkv_a2[l], rwkv_g2[l], rwkv_k_k[l],
                                         rwkv_k_a[l], rwkv_r_k[l], rwkv_ln_w[l], rwkv_ln_b[l], v_res)
        y_fox = forgetting_attention(fq, fk, fv, fg, ff, fox_q_g[l], fox_k_g[l], fox_f_b[l])
        y_pool = multiscale_pool(pool_in, pool_w[l], pool_scale[l])
        mixed = jnp.concatenate([y_rwkv, y_fox, y_pool], axis=-1) @ w_out[l]
        x = x + rms_norm(mixed, mix_post_g[l])
        h = rms_norm(x, ffn_pre_g[l])
        gate, up = jnp.split(h @ ffn_w_gu[l], 2, axis=-1)
        f = (jax.nn.silu(gate) * up) @ ffn_w_down[l]
        x = x + rms_norm(f, ffn_post_g[l])
    return x
```

```python
import functools

import jax
import jax.numpy as jnp
from jax import lax
from jax.experimental import pallas as pl
from jax.experimental.pallas import tpu as pltpu

F32 = jnp.float32
BF16 = jnp.bfloat16

D_MODEL = 1024
HEAD_DIM = 64
PAIR = 2 * HEAD_DIM
N_HEADS = 6
N_PAIRS = N_HEADS // 2
MIX_W = N_HEADS * HEAD_DIM
POOL_W = 256
POOL_HALO = 16
FFN_HIDDEN = 2816
FFN_CHUNK = 256
RMS_EPS = 1e-6
GN_EPS = HEAD_DIM * 1e-5

RWKV_IN = 1440
RWKV_PAD = 1536
FOX_IN = 4 * MIX_W
FF_PAD = 128
IN_SPLITS = ((0, 1536), (1536, 3072), (3072, 3200), (3200, 3456))
IN_TOTAL = 3456

SCAN_CHUNK = 64
VMEM_LIMIT = 56 * 1024 * 1024


def _params(*sem):
    return pltpu.CompilerParams(dimension_semantics=sem, vmem_limit_bytes=VMEM_LIMIT)


def _dot(a, b):
    return jnp.dot(a.astype(BF16), b.astype(BF16), preferred_element_type=F32)


def _dot_nt(a, b):
    return lax.dot_general(a.astype(BF16), b.astype(BF16), (((1,), (1,)), ((), ())),
                           preferred_element_type=F32)


def _split3(x):
    hi = x.astype(BF16)
    r1 = x - hi.astype(F32)
    mid = r1.astype(BF16)
    lo = (r1 - mid.astype(F32)).astype(BF16)
    return hi, mid, lo


def _dot3(x, w):
    hi, mid, lo = _split3(x)
    d = lambda t: jnp.dot(t, w, preferred_element_type=F32)
    return d(hi) + d(mid) + d(lo)


def _dot3r(w, x):
    hi, mid, lo = _split3(x)
    d = lambda t: jnp.dot(w, t, preferred_element_type=F32)
    return d(hi) + d(mid) + d(lo)


def _rms(x, g):
    ms = jnp.mean(x * x, axis=-1, keepdims=True)
    return x * lax.rsqrt(ms + RMS_EPS) * g


def _softplus(z):
    return jnp.maximum(z, 0.0) + jnp.log1p(jnp.exp(-jnp.abs(z)))


def _sigmoid(z):
    return 1.0 / (1.0 + jnp.exp(-z))


def _tri(n, inclusive, dtype):
    r = lax.broadcasted_iota(jnp.int32, (n, n), 0)
    c = lax.broadcasted_iota(jnp.int32, (n, n), 1)
    keep = (r >= c) if inclusive else (r > c)
    return jnp.where(keep, 1.0, 0.0).astype(dtype)


def _inproj_body(x_ref, g_ref, w_ref, o_rwkv, o_fox, o_ff, o_pool):
    h = _rms(x_ref[...], g_ref[...]).astype(BF16)
    for o_ref, (c0, c1) in zip((o_rwkv, o_fox, o_ff, o_pool), IN_SPLITS):
        o_ref[...] = jnp.dot(h, w_ref[:, c0:c1], preferred_element_type=F32)


def _inproj(x, g, w, tm=256):
    n = x.shape[0]
    widths = [c1 - c0 for c0, c1 in IN_SPLITS]
    return pl.pallas_call(
        _inproj_body,
        grid=(n // tm,),
        in_specs=[pl.BlockSpec((tm, D_MODEL), lambda i: (i, 0)),
                  pl.BlockSpec((1, D_MODEL), lambda i: (0, 0)),
                  pl.BlockSpec((D_MODEL, IN_TOTAL), lambda i: (0, 0))],
        out_specs=[pl.BlockSpec((tm, w_), lambda i: (i, 0)) for w_ in widths],
        out_shape=[jax.ShapeDtypeStruct((n, w_), F32) for w_ in widths],
        compiler_params=_params("parallel"),
        name="inproj",
    )(x, g, w)


def _rwkv_prep_body(seq_len, has_vres, *refs):
    if has_vres:
        (f_ref, halo_ref, mu_ref, vec_ref, w2_ref, a2_ref, g2_ref, ones_ref, vf_ref, v1_ref, v2_ref,
         r_o, k_o, v_o, lw_o, kk_o, b_o, g_o, bonus_o) = refs
    else:
        (f_ref, halo_ref, mu_ref, vec_ref, w2_ref, a2_ref, g2_ref, ones_ref,
         r_o, k_o, v_o, lw_o, kk_o, b_o, g_o, bonus_o) = refs
    tm = f_ref.shape[0]
    at_start = lax.rem(pl.program_id(0) * tm, seq_len) == 0
    f = f_ref[...]
    halo = jnp.where(at_start, 0.0, halo_ref[7:8, :])
    row = lax.broadcasted_iota(jnp.int32, (tm, 1), 0)
    prev = jnp.where(row == 0, halo, pltpu.roll(f, 1, axis=0))
    f = f + (prev - f) * mu_ref[...]
    r, k, v = f[:, 0:384], f[:, 384:768], f[:, 768:1152]
    wa = f[:, 1152:1280]
    gd = f[:, 1280:1536]
    w0, a0, k_k, k_a, r_k, v0 = (vec_ref[j:j + 1, :] for j in range(6))

    w_log = -_softplus(-(w0 + _dot(jnp.tanh(wa), w2_ref[...]))) - 0.5
    lw_o[...] = -jnp.exp(w_log)
    a = _sigmoid(a0 + _dot(wa, a2_ref[...]))
    g_o[...] = _dot(_sigmoid(gd), g2_ref[...])
    if has_vres:
        gate = _sigmoid(v0 + _dot(_dot(v, v1_ref[...]), v2_ref[...]))
        v = v + (vf_ref[...] - v) * gate
    kk = k * k_k
    ss = _dot3(kk * kk, ones_ref[...])
    kk = kk / jnp.maximum(jnp.sqrt(ss), 1e-12)
    k = k * (1.0 + (a - 1.0) * k_a)
    r_o[...] = r
    k_o[...] = k
    v_o[...] = v
    kk_o[...] = kk
    b_o[...] = kk * a
    bonus_o[...] = _dot3(r * k * r_k, ones_ref[...]) * v


def _rwkv_prep(seq_len, feats, mu, vecs, w2p, a2p, g2p, ones, vres, tm=256):
    n = feats.shape[0]
    full = lambda a: pl.BlockSpec(a.shape, lambda i: (0,) * a.ndim)
    tile = pl.BlockSpec((tm, MIX_W), lambda i: (i, 0))
    in_specs = [pl.BlockSpec((tm, RWKV_PAD), lambda i: (i, 0)),
                pl.BlockSpec((8, RWKV_PAD), lambda i: (jnp.maximum(i * (tm // 8) - 1, 0), 0)),
                full(mu), full(vecs), full(w2p), full(a2p), full(g2p), full(ones)]
    args = [feats, feats, mu, vecs, w2p, a2p, g2p, ones]
    if vres is not None:
        v_first, v1p, v2p = vres
        in_specs += [tile, full(v1p), full(v2p)]
        args += [v_first, v1p, v2p]
    return pl.pallas_call(
        functools.partial(_rwkv_prep_body, seq_len, vres is not None),
        grid=(n // tm,),
        in_specs=in_specs,
        out_specs=[tile] * 8,
        out_shape=[jax.ShapeDtypeStruct((n, MIX_W), F32)] * 8,
        compiler_params=_params("parallel"),
        name="rwkv_prep",
    )(*args)


def _scan_body(chunks_per_step, r_ref, lw_ref, k_ref, v_ref, kk_ref, b_ref, y_ref, h_ref):
    c_len = SCAN_CHUNK

    @pl.when(pl.program_id(2) == 0)
    def _():
        h_ref[...] = jnp.zeros_like(h_ref)

    lane = lax.broadcasted_iota(jnp.int32, (1, PAIR), 1)
    left = lane < HEAD_DIM
    ri = lax.broadcasted_iota(jnp.int32, (PAIR, PAIR), 0)
    ci = lax.broadcasted_iota(jnp.int32, (PAIR, PAIR), 1)
    strict, incl, diag = ri > ci, ri >= ci, ri == ci
    tri = _tri(c_len, True, BF16)
    eye = jnp.where(diag, 1.0, 0.0)

    def stack(x):
        return jnp.concatenate([jnp.where(left, x, 0.0), jnp.where(left, 0.0, x)], axis=0)

    def chunk(c, carry):
        rows = pl.ds(pl.multiple_of(c * c_len, c_len), c_len)
        lw = lw_ref[rows, :]
        cum = _dot3r(tri, lw)
        cum_end = cum[c_len - 1:c_len, :]
        e_neg = jnp.exp(-cum)
        e_tail = jnp.exp(cum_end - cum)
        kk, b, k = kk_ref[rows, :], b_ref[rows, :], k_ref[rows, :]
        l_a = stack(-kk * jnp.exp(cum - lw))
        l_r = stack(r_ref[rows, :] * jnp.exp(cum))
        r_b, r_k = stack(b * e_neg), stack(k * e_neg)
        rb_end, rk_end = stack(b * e_tail), stack(k * e_tail)
        v_s = stack(v_ref[rows, :])

        a_ab = jnp.where(strict, _dot_nt(l_a, r_b), 0.0)
        a_ak = jnp.where(strict, _dot_nt(l_a, r_k), 0.0)
        a_rb = jnp.where(incl, _dot_nt(l_r, r_b), 0.0)
        a_rk = jnp.where(incl, _dot_nt(l_r, r_k), 0.0)

        x = a_ab
        t_inv = eye + x
        for _ in range(5):
            x = _dot(x, x)
            t_inv = t_inv + _dot(t_inv, x)

        t_a = _dot(t_inv, l_a)
        w_ = _dot(t_inv, _dot(a_ak, v_s))
        r_q = l_r + _dot(a_rb, t_a)
        y_0 = _dot(a_rb, w_) + _dot(a_rk, v_s)
        rb_t, rk_t = rb_end.T, rk_end.T
        m_ = jnp.where(diag, jnp.exp(cum_end), 0.0) + _dot(rb_t, t_a)
        n_ = _dot(rb_t, w_) + _dot(rk_t, v_s)

        h = h_ref[...]
        y_st = _dot(r_q, h) + y_0
        y_ref[rows, :] = y_st[:c_len] + y_st[c_len:]
        h_ref[...] = _dot(m_, h) + n_
        return carry

    lax.fori_loop(0, chunks_per_step, chunk, 0)


def _rwkv_scan(seq_len, r, lw, k, v, kk, b, chunks_per_step=4):
    n = r.shape[0]
    rows = chunks_per_step * SCAN_CHUNK
    steps = seq_len // rows
    spec = pl.BlockSpec((rows, PAIR), lambda bi, p, j: (bi * steps + j, p))
    return pl.pallas_call(
        functools.partial(_scan_body, chunks_per_step),
        grid=(n // seq_len, N_PAIRS, steps),
        in_specs=[spec] * 6,
        out_specs=spec,
        out_shape=jax.ShapeDtypeStruct((n, MIX_W), F32),
        scratch_shapes=[pltpu.VMEM((PAIR, PAIR), F32)],
        compiler_params=_params("parallel", "parallel", "arbitrary"),
        name="rwkv_scan",
    )(r, lw, k, v, kk, b)


def _fox_prep_body(seq_len, q_ref, k_ref, v_ref, ff_ref, qg_ref, kg_ref, fb_ref, ones_ref,
                   q_o, k_o, v_o, c_o, ct_o, carry_ref):
    tm = q_ref.shape[0]

    @pl.when(lax.rem(pl.program_id(0) * tm, seq_len) == 0)
    def _():
        carry_ref[...] = jnp.zeros_like(carry_ref)

    def qk_norm(x, g):
        ms = _dot3(x * x, ones_ref[...]) * (1.0 / HEAD_DIM)
        return x * lax.rsqrt(ms + RMS_EPS) * g

    q_o[...] = (qk_norm(q_ref[...], qg_ref[...]) * (HEAD_DIM ** -0.5)).astype(BF16)
    k_o[...] = qk_norm(k_ref[...], kg_ref[...]).astype(BF16)
    v_o[...] = v_ref[...].astype(BF16)
    log_f = -_softplus(-(ff_ref[...] + fb_ref[...]))
    cum = _dot3r(_tri(tm, True, BF16), log_f) + carry_ref[...]
    carry_ref[...] = cum[tm - 1:tm, :]
    c_o[...] = cum
    ct_o[0] = cum.T[0:8, :]


def _fox_prep(seq_len, fox, ff, qg, kg, fb, ones, tm=256):
    n = fox.shape[0]
    full = lambda a: pl.BlockSpec(a.shape, lambda i: (0,) * a.ndim)
    col = lambda j: pl.BlockSpec((tm, MIX_W), lambda i: (i, j))
    tile = pl.BlockSpec((tm, MIX_W), lambda i: (i, 0))
    return pl.pallas_call(
        functools.partial(_fox_prep_body, seq_len),
        grid=(n // tm,),
        in_specs=[col(0), col(1), col(2), pl.BlockSpec((tm, FF_PAD), lambda i: (i, 0)),
                  full(qg), full(kg), full(fb), full(ones)],
        out_specs=[tile, tile, tile, pl.BlockSpec((tm, FF_PAD), lambda i: (i, 0)),
                   pl.BlockSpec((1, 8, tm), lambda i: (i, 0, 0))],
        out_shape=[jax.ShapeDtypeStruct((n, MIX_W), BF16)] * 3
        + [jax.ShapeDtypeStruct((n, FF_PAD), F32), jax.ShapeDtypeStruct((n // tm, 8, tm), F32)],
        scratch_shapes=[pltpu.VMEM((1, FF_PAD), F32)],
        compiler_params=_params("arbitrary"),
        name="fox_prep",
    )(fox, fox, fox, ff, qg, kg, fb, ones)


def _fox_attn_body(q_ref, k_ref, v_ref, cc_ref, ct_ref, o_ref):
    p = pl.program_id(1)
    i = pl.program_id(2)
    t = q_ref.shape[0]
    lane = lax.broadcasted_iota(jnp.int32, (1, PAIR), 1)
    left = lane < HEAD_DIM
    q = q_ref[...]
    q_h = (jnp.where(left, q, jnp.zeros_like(q)), jnp.where(left, jnp.zeros_like(q), q))
    cc = cc_ref[...]
    sub = lax.broadcasted_iota(jnp.int32, (8, 1), 0)
    c_q = [jnp.sum(jnp.where(lane == 2 * p + h, cc, 0.0), axis=1, keepdims=True) for h in (0, 1)]

    def block(j, carry, diagonal):
        start = pl.multiple_of(j * t, t)
        k_j = k_ref[pl.ds(start, t), :]
        v_j = v_ref[pl.ds(start, t), :]
        ct = ct_ref[j]
        acc = carry[4]
        new = []
        upd = []
        for h in (0, 1):
            m, l = carry[2 * h], carry[2 * h + 1]
            c_k = jnp.sum(jnp.where(sub == 2 * p + h, ct, 0.0), axis=0, keepdims=True)
            s = _dot_nt(q_h[h], k_j) + (c_q[h] - c_k)
            if diagonal:
                rr = lax.broadcasted_iota(jnp.int32, (t, t), 0)
                cc_ = lax.broadcasted_iota(jnp.int32, (t, t), 1)
                s = jnp.where(cc_ <= rr, s, -1e30)
            m_new = jnp.maximum(m, jnp.max(s, axis=1, keepdims=True))
            pr = jnp.exp(s - m_new)
            alpha = jnp.exp(m - m_new)
            new += [m_new, l * alpha + jnp.sum(pr, axis=1, keepdims=True)]
            upd.append(acc * alpha + jnp.dot(pr.astype(BF16), v_j, preferred_element_type=F32))
        return (*new, jnp.where(left, upd[0], upd[1]))

    neg = jnp.full((t, 1), -1e30, F32)
    zero = jnp.zeros((t, 1), F32)
    carry = block(i, (neg, zero, neg, zero, jnp.zeros((t, PAIR), F32)), True)
    carry = lax.fori_loop(0, i, lambda j, c: block(j, c, False), carry)
    o_ref[...] = jnp.where(left, carry[4] / carry[1], carry[4] / carry[3])


def _fox_attn(seq_len, q, k, v, cc, ct, t=256):
    n = q.shape[0]
    nq = seq_len // t
    return pl.pallas_call(
        _fox_attn_body,
        grid=(n // seq_len, N_PAIRS, nq),
        in_specs=[pl.BlockSpec((t, PAIR), lambda bi, p, i: (bi * nq + i, p)),
                  pl.BlockSpec((seq_len, PAIR), lambda bi, p, i: (bi, p)),
                  pl.BlockSpec((seq_len, PAIR), lambda bi, p, i: (bi, p)),
                  pl.BlockSpec((t, FF_PAD), lambda bi, p, i: (bi * nq + i, 0)),
                  pl.BlockSpec((nq, 8, t), lambda bi, p, i: (bi, 0, 0))],
        out_specs=pl.BlockSpec((t, PAIR), lambda bi, p, i: (bi * nq + i, p)),
        out_shape=jax.ShapeDtypeStruct((n, MIX_W), F32),
        compiler_params=_params("parallel", "parallel", "arbitrary"),
        name="fox_attn",
    )(q, k, v, cc, ct)


def _mix_out_body(seq_len, x_ref, ys_ref, g_ref, bonus_ref, at_ref, fg_ref, pool_ref, halo_ref,
                  ln_ref, ones_ref, pw_ref, ps_ref, wo_ref, pg_ref, o_ref):
    tm = x_ref.shape[0]
    row0 = lax.rem(pl.program_id(0) * tm, seq_len)
    ones = ones_ref[...]

    y = ys_ref[...]
    d = y - _dot3(y, ones) * (1.0 / HEAD_DIM)
    var = _dot3(d * d, ones) * (1.0 / HEAD_DIM)
    y_rwkv = (d * lax.rsqrt(var + GN_EPS) * ln_ref[0:1, :] + ln_ref[1:2, :] + bonus_ref[...]) * g_ref[...]

    y_fox = at_ref[...] * _sigmoid(fg_ref[...])

    pin = pool_ref[...]
    halo = jnp.where(row0 == 0, 0.0, halo_ref[...])
    s = jnp.concatenate([halo, pin], axis=0)
    sums = []
    for shift in (1, 2, 4, 8):
        s = s + pltpu.roll(s, shift, axis=0)
        sums.append(s[POOL_HALO:, :])
    grp = lax.broadcasted_iota(jnp.int32, (1, POOL_W), 1) // HEAD_DIM
    win_sum = jnp.where(grp == 0, sums[0], jnp.where(grp == 1, sums[1], jnp.where(grp == 2, sums[2], sums[3])))
    win = jnp.where(grp == 0, 2.0, jnp.where(grp == 1, 4.0, jnp.where(grp == 2, 8.0, 16.0)))
    pos = (row0 + lax.broadcasted_iota(jnp.int32, (tm, 1), 0) + 1).astype(F32)
    u = win_sum / jnp.minimum(pos, win) - pin
    y_pool = _dot(u, pw_ref[...]) * ps_ref[...]

    mixed = (_dot(y_rwkv, wo_ref[0:384, :]) + _dot(y_fox, wo_ref[384:768, :])
             + _dot(y_pool, wo_ref[768:1024, :]))
    o_ref[...] = x_ref[...] + _rms(mixed, pg_ref[...])


def _mix_out(seq_len, x, y_scan, g, bonus, attn, fox, pool_in, ln, ones, pw, ps, wo, pg, tm=256):
    n = x.shape[0]
    full = lambda a: pl.BlockSpec(a.shape, lambda i: (0,) * a.ndim)
    tile = pl.BlockSpec((tm, MIX_W), lambda i: (i, 0))
    return pl.pallas_call(
        functools.partial(_mix_out_body, seq_len),
        grid=(n // tm,),
        in_specs=[pl.BlockSpec((tm, D_MODEL), lambda i: (i, 0)), tile, tile, tile, tile,
                  pl.BlockSpec((tm, MIX_W), lambda i: (i, 3)),
                  pl.BlockSpec((tm, POOL_W), lambda i: (i, 0)),
                  pl.BlockSpec((POOL_HALO, POOL_W),
                               lambda i: (jnp.maximum(i * (tm // POOL_HALO) - 1, 0), 0)),
                  full(ln), full(ones), full(pw), full(ps), full(wo), full(pg)],
        out_specs=pl.BlockSpec((tm, D_MODEL), lambda i: (i, 0)),
        out_shape=jax.ShapeDtypeStruct((n, D_MODEL), F32),
        compiler_params=_params("parallel"),
        name="mix_out",
    )(x, y_scan, g, bonus, attn, fox, pool_in, pool_in, ln, ones, pw, ps, wo, pg)


def _ffn_body(x_ref, g1_ref, wgu_ref, wd_ref, g2_ref, o_ref):
    x = x_ref[...]
    h = _rms(x, g1_ref[...]).astype(BF16)
    acc = jnp.zeros(x.shape, F32)
    for c in range(FFN_HIDDEN // FFN_CHUNK):
        c0 = c * FFN_CHUNK
        gate = jnp.dot(h, wgu_ref[:, c0:c0 + FFN_CHUNK], preferred_element_type=F32)
        up = jnp.dot(h, wgu_ref[:, FFN_HIDDEN + c0:FFN_HIDDEN + c0 + FFN_CHUNK],
                     preferred_element_type=F32)
        act = (gate * _sigmoid(gate) * up).astype(BF16)
        acc = acc + jnp.dot(act, wd_ref[c0:c0 + FFN_CHUNK, :], preferred_element_type=F32)
    o_ref[...] = x + _rms(acc, g2_ref[...])


def _ffn(x, g1, wgu, wd, g2, tm=256):
    n = x.shape[0]
    full = lambda a: pl.BlockSpec(a.shape, lambda i: (0,) * a.ndim)
    tile = pl.BlockSpec((tm, D_MODEL), lambda i: (i, 0))
    return pl.pallas_call(
        _ffn_body,
        grid=(n // tm,),
        in_specs=[tile, full(g1), full(wgu), full(wd), full(g2)],
        out_specs=tile,
        out_shape=jax.ShapeDtypeStruct((n, D_MODEL), F32),
        compiler_params=_params("parallel"),
        name="ffn",
    )(x, g1, wgu, wd, g2)


def _pad_rows(a, rows, at=0):
    out = jnp.zeros((rows, a.shape[1]), a.dtype)
    return lax.dynamic_update_slice(out, a, (at, 0))


def _block_diag_ones():
    h = jnp.arange(MIX_W) // HEAD_DIM
    return (h[:, None] == h[None, :]).astype(BF16)


def kernel(x, mix_pre_g, mix_post_g, ffn_pre_g, ffn_post_g, w_in, w_out, rwkv_mu, rwkv_w0, rwkv_w2,
           rwkv_a0, rwkv_a2, rwkv_g2, rwkv_v0, rwkv_v1, rwkv_v2, rwkv_k_k, rwkv_k_a, rwkv_r_k,
           rwkv_ln_w, rwkv_ln_b, fox_q_g, fox_k_g, fox_f_b, pool_w, pool_scale, ffn_w_gu, ffn_w_down):
    batch, seq_len, _ = x.shape
    n = batch * seq_len
    depth = w_in.shape[0]
    ones = _block_diag_ones()
    row = lambda a: a.reshape(1, -1).astype(F32)
    xf = x.reshape(n, D_MODEL)
    v_first = None
    for l in range(depth):
        w = w_in[l]
        z = lambda c: jnp.zeros((D_MODEL, c), w.dtype)
        w_arr = jnp.concatenate([w[:, :RWKV_IN], z(RWKV_PAD - RWKV_IN), w[:, RWKV_IN:RWKV_IN + FOX_IN],
                                 w[:, 2976:2982], z(FF_PAD - N_HEADS), w[:, 2982:]], axis=1).astype(BF16)
        feats, fox, ff, pool_in = _inproj(xf, row(mix_pre_g[l]), w_arr)

        mu = jnp.pad(rwkv_mu[l], (0, RWKV_PAD - RWKV_IN)).reshape(1, RWKV_PAD)
        v0 = rwkv_v0[l - 1] if l > 0 else jnp.zeros((MIX_W,), F32)
        vecs = jnp.stack([rwkv_w0[l], rwkv_a0[l], rwkv_k_k[l], rwkv_k_a[l], rwkv_r_k[l].reshape(-1), v0,
                          jnp.zeros((MIX_W,), F32), jnp.zeros((MIX_W,), F32)])
        w2p = _pad_rows(rwkv_w2[l], 128, 0).astype(BF16)
        a2p = _pad_rows(rwkv_a2[l], 128, 64).astype(BF16)
        g2p = _pad_rows(rwkv_g2[l], 256, 0).astype(BF16)
        vres = None
        if l > 0:
            v1p = jnp.pad(rwkv_v1[l - 1], ((0, 0), (0, 128 - 32))).astype(BF16)
            v2p = _pad_rows(rwkv_v2[l - 1], 128, 0).astype(BF16)
            vres = (v_first, v1p, v2p)
        r, k, v, lw, kk, b, g, bonus = _rwkv_prep(seq_len, feats, mu, vecs, w2p, a2p, g2p, ones, vres)
        if l == 0:
            v_first = v
        y_scan = _rwkv_scan(seq_len, r, lw, k, v, kk, b)

        tile6 = lambda a: jnp.tile(a, N_HEADS).reshape(1, MIX_W)
        fb = jnp.pad(fox_f_b[l], (0, FF_PAD - N_HEADS)).reshape(1, FF_PAD)
        q_n, k_n, v_b, cc, ct = _fox_prep(seq_len, fox, ff, tile6(fox_q_g[l]), tile6(fox_k_g[l]), fb, ones)
        attn = _fox_attn(seq_len, q_n, k_n, v_b, cc, ct)

        ln = jnp.stack([rwkv_ln_w[l], rwkv_ln_b[l]] + [jnp.zeros((MIX_W,), F32)] * 6)
        pw = jax.scipy.linalg.block_diag(*[pool_w[l, gi] for gi in range(4)]).astype(BF16)
        xf = _mix_out(seq_len, xf, y_scan, g, bonus, attn, fox, pool_in, ln, ones, pw,
                      row(pool_scale[l]), w_out[l].astype(BF16), row(mix_post_g[l]))
        xf = _ffn(xf, row(ffn_pre_g[l]), ffn_w_gu[l].astype(BF16), ffn_w_down[l].astype(BF16),
                  row(ffn_post_g[l]))
    return xf.reshape(batch, seq_len, D_MODEL)
```

```python
import functools

import jax
import jax.numpy as jnp
from jax import lax
from jax.experimental import pallas as pl
from jax.experimental.pallas import tpu as pltpu

F32 = jnp.float32
BF16 = jnp.bfloat16

D_MODEL = 1024
HEAD_DIM = 64
PAIR = 2 * HEAD_DIM
N_HEADS = 6
N_PAIRS = N_HEADS // 2
MIX_W = N_HEADS * HEAD_DIM
POOL_W = 256
POOL_HALO = 16
FFN_HIDDEN = 2816
FFN_CHUNK = 256
RMS_EPS = 1e-6
GN_EPS = HEAD_DIM * 1e-5

RWKV_IN = 1440
RWKV_PAD = 1536
FOX_IN = 4 * MIX_W
FF_PAD = 128
IN_SPLITS = ((0, 1536), (1536, 3072), (3072, 3200), (3200, 3456))
IN_TOTAL = 3456

LOG2E = 1.4426950408889634
FOX_BLOCK = 256
UNDERFLOW_CUT = 110.0
BF16_MARGIN = 1.02
SCAN_CHUNK = 64
VMEM_LIMIT = 56 * 1024 * 1024


def _params(*sem):
    return pltpu.CompilerParams(dimension_semantics=sem, vmem_limit_bytes=VMEM_LIMIT)


def _dot(a, b):
    return jnp.dot(a.astype(BF16), b.astype(BF16), preferred_element_type=F32)


def _dot_nt(a, b):
    return lax.dot_general(a.astype(BF16), b.astype(BF16), (((1,), (1,)), ((), ())),
                           preferred_element_type=F32)


def _split3(x):
    hi = x.astype(BF16)
    r1 = x - hi.astype(F32)
    mid = r1.astype(BF16)
    lo = (r1 - mid.astype(F32)).astype(BF16)
    return hi, mid, lo


def _dot3(x, w):
    hi, mid, lo = _split3(x)
    d = lambda t: jnp.dot(t, w, preferred_element_type=F32)
    return d(hi) + d(mid) + d(lo)


def _dot3r(w, x):
    hi, mid, lo = _split3(x)
    d = lambda t: jnp.dot(w, t, preferred_element_type=F32)
    return d(hi) + d(mid) + d(lo)


def _rms(x, g):
    ms = jnp.mean(x * x, axis=-1, keepdims=True)
    return x * lax.rsqrt(ms + RMS_EPS) * g


def _softplus(z):
    return jnp.maximum(z, 0.0) + jnp.log1p(jnp.exp(-jnp.abs(z)))


def _sigmoid(z):
    return 1.0 / (1.0 + jnp.exp(-z))


def _tri(n, inclusive, dtype):
    r = lax.broadcasted_iota(jnp.int32, (n, n), 0)
    c = lax.broadcasted_iota(jnp.int32, (n, n), 1)
    keep = (r >= c) if inclusive else (r > c)
    return jnp.where(keep, 1.0, 0.0).astype(dtype)


def _inproj_body(x_ref, g_ref, w_ref, o_rwkv, o_fox, o_ff, o_pool):
    h = _rms(x_ref[...], g_ref[...]).astype(BF16)
    for o_ref, (c0, c1) in zip((o_rwkv, o_fox, o_ff, o_pool), IN_SPLITS):
        o_ref[...] = jnp.dot(h, w_ref[:, c0:c1], preferred_element_type=F32)


def _inproj(x, g, w, tm=256):
    n = x.shape[0]
    widths = [c1 - c0 for c0, c1 in IN_SPLITS]
    return pl.pallas_call(
        _inproj_body,
        grid=(n // tm,),
        in_specs=[pl.BlockSpec((tm, D_MODEL), lambda i: (i, 0)),
                  pl.BlockSpec((1, D_MODEL), lambda i: (0, 0)),
                  pl.BlockSpec((D_MODEL, IN_TOTAL), lambda i: (0, 0))],
        out_specs=[pl.BlockSpec((tm, w_), lambda i: (i, 0)) for w_ in widths],
        out_shape=[jax.ShapeDtypeStruct((n, w_), F32) for w_ in widths],
        compiler_params=_params("parallel"),
        name="inproj",
    )(x, g, w)


def _rwkv_prep_body(seq_len, has_vres, *refs):
    if has_vres:
        (f_ref, halo_ref, mu_ref, vec_ref, w2_ref, a2_ref, g2_ref, ones_ref, vf_ref, v1_ref, v2_ref,
         r_o, k_o, v_o, lw_o, kk_o, b_o, g_o, bonus_o) = refs
    else:
        (f_ref, halo_ref, mu_ref, vec_ref, w2_ref, a2_ref, g2_ref, ones_ref,
         r_o, k_o, v_o, lw_o, kk_o, b_o, g_o, bonus_o) = refs
    tm = f_ref.shape[0]
    at_start = lax.rem(pl.program_id(0) * tm, seq_len) == 0
    f = f_ref[...]
    halo = jnp.where(at_start, 0.0, halo_ref[7:8, :])
    row = lax.broadcasted_iota(jnp.int32, (tm, 1), 0)
    prev = jnp.where(row == 0, halo, pltpu.roll(f, 1, axis=0))
    f = f + (prev - f) * mu_ref[...]
    r, k, v = f[:, 0:384], f[:, 384:768], f[:, 768:1152]
    wa = f[:, 1152:1280]
    gd = f[:, 1280:1536]
    w0, a0, k_k, k_a, r_k, v0 = (vec_ref[j:j + 1, :] for j in range(6))

    w_log = -_softplus(-(w0 + _dot(jnp.tanh(wa), w2_ref[...]))) - 0.5
    lw_o[...] = -jnp.exp(w_log)
    a = _sigmoid(a0 + _dot(wa, a2_ref[...]))
    g_o[...] = _dot(_sigmoid(gd), g2_ref[...])
    if has_vres:
        gate = _sigmoid(v0 + _dot(_dot(v, v1_ref[...]), v2_ref[...]))
        v = v + (vf_ref[...] - v) * gate
    kk = k * k_k
    ss = _dot3(kk * kk, ones_ref[...])
    kk = kk / jnp.maximum(jnp.sqrt(ss), 1e-12)
    k = k * (1.0 + (a - 1.0) * k_a)
    r_o[...] = r
    k_o[...] = k
    v_o[...] = v
    kk_o[...] = kk
    b_o[...] = kk * a
    bonus_o[...] = _dot3(r * k * r_k, ones_ref[...]) * v


def _rwkv_prep(seq_len, feats, mu, vecs, w2p, a2p, g2p, ones, vres, tm=256):
    n = feats.shape[0]
    full = lambda a: pl.BlockSpec(a.shape, lambda i: (0,) * a.ndim)
    tile = pl.BlockSpec((tm, MIX_W), lambda i: (i, 0))
    in_specs = [pl.BlockSpec((tm, RWKV_PAD), lambda i: (i, 0)),
                pl.BlockSpec((8, RWKV_PAD), lambda i: (jnp.maximum(i * (tm // 8) - 1, 0), 0)),
                full(mu), full(vecs), full(w2p), full(a2p), full(g2p), full(ones)]
    args = [feats, feats, mu, vecs, w2p, a2p, g2p, ones]
    if vres is not None:
        v_first, v1p, v2p = vres
        in_specs += [tile, full(v1p), full(v2p)]
        args += [v_first, v1p, v2p]
    return pl.pallas_call(
        functools.partial(_rwkv_prep_body, seq_len, vres is not None),
        grid=(n // tm,),
        in_specs=in_specs,
        out_specs=[tile] * 8,
        out_shape=[jax.ShapeDtypeStruct((n, MIX_W), F32)] * 8,
        compiler_params=_params("parallel"),
        name="rwkv_prep",
    )(*args)


def _scan_body(chunks_per_step, r_ref, lw_ref, k_ref, v_ref, kk_ref, b_ref, y_ref, h_ref):
    c_len = SCAN_CHUNK

    @pl.when(pl.program_id(2) == 0)
    def _():
        h_ref[...] = jnp.zeros_like(h_ref)

    lane = lax.broadcasted_iota(jnp.int32, (1, PAIR), 1)
    left = lane < HEAD_DIM
    ri = lax.broadcasted_iota(jnp.int32, (PAIR, PAIR), 0)
    ci = lax.broadcasted_iota(jnp.int32, (PAIR, PAIR), 1)
    strict, incl, diag = ri > ci, ri >= ci, ri == ci
    tri = _tri(c_len, True, BF16)
    eye = jnp.where(diag, 1.0, 0.0)

    def stack(x):
        return jnp.concatenate([jnp.where(left, x, 0.0), jnp.where(left, 0.0, x)], axis=0)

    def chunk(c, carry):
        rows = pl.ds(pl.multiple_of(c * c_len, c_len), c_len)
        lw = lw_ref[rows, :]
        cum = _dot3r(tri, lw)
        cum_end = cum[c_len - 1:c_len, :]
        e_neg = jnp.exp(-cum)
        e_tail = jnp.exp(cum_end - cum)
        kk, b, k = kk_ref[rows, :], b_ref[rows, :], k_ref[rows, :]
        l_a = stack(-kk * jnp.exp(cum - lw))
        l_r = stack(r_ref[rows, :] * jnp.exp(cum))
        r_b, r_k = stack(b * e_neg), stack(k * e_neg)
        rb_end, rk_end = stack(b * e_tail), stack(k * e_tail)
        v_s = stack(v_ref[rows, :])

        a_ab = jnp.where(strict, _dot_nt(l_a, r_b), 0.0)
        a_ak = jnp.where(strict, _dot_nt(l_a, r_k), 0.0)
        a_rb = jnp.where(incl, _dot_nt(l_r, r_b), 0.0)
        a_rk = jnp.where(incl, _dot_nt(l_r, r_k), 0.0)

        x = a_ab
        t_inv = eye + x
        for _ in range(5):
            x = _dot(x, x)
            t_inv = t_inv + _dot(t_inv, x)

        t_a = _dot(t_inv, l_a)
        w_ = _dot(t_inv, _dot(a_ak, v_s))
        r_q = l_r + _dot(a_rb, t_a)
        y_0 = _dot(a_rb, w_) + _dot(a_rk, v_s)
        rb_t, rk_t = rb_end.T, rk_end.T
        m_ = jnp.where(diag, jnp.exp(cum_end), 0.0) + _dot(rb_t, t_a)
        n_ = _dot(rb_t, w_) + _dot(rk_t, v_s)

        h = h_ref[...]
        y_st = _dot(r_q, h) + y_0
        y_ref[rows, :] = y_st[:c_len] + y_st[c_len:]
        h_ref[...] = _dot(m_, h) + n_
        return carry

    lax.fori_loop(0, chunks_per_step, chunk, 0)


def _rwkv_scan(seq_len, r, lw, k, v, kk, b, chunks_per_step=4):
    n = r.shape[0]
    rows = chunks_per_step * SCAN_CHUNK
    steps = seq_len // rows
    spec = pl.BlockSpec((rows, PAIR), lambda bi, p, j: (bi * steps + j, p))
    return pl.pallas_call(
        functools.partial(_scan_body, chunks_per_step),
        grid=(n // seq_len, N_PAIRS, steps),
        in_specs=[spec] * 6,
        out_specs=spec,
        out_shape=jax.ShapeDtypeStruct((n, MIX_W), F32),
        scratch_shapes=[pltpu.VMEM((PAIR, PAIR), F32)],
        compiler_params=_params("parallel", "parallel", "arbitrary"),
        name="rwkv_scan",
    )(r, lw, k, v, kk, b)


def _fox_prep_body(seq_len, q_ref, k_ref, v_ref, ff_ref, qg_ref, kg_ref, fb_ref, ones_ref,
                   q_o, k_o, v_o, ct_o, cend_o, carry_ref):
    tm = q_ref.shape[0]
    blocks = seq_len // tm
    j = lax.rem(pl.program_id(0), blocks)

    @pl.when(j == 0)
    def _():
        carry_ref[...] = jnp.zeros_like(carry_ref)
        cend_o[...] = jnp.zeros_like(cend_o)

    def qk_norm(x, g):
        ms = _dot3(x * x, ones_ref[...]) * (1.0 / HEAD_DIM)
        return x * lax.rsqrt(ms + RMS_EPS) * g

    q_o[...] = (qk_norm(q_ref[...], qg_ref[...]) * (LOG2E * HEAD_DIM ** -0.5)).astype(BF16)
    k_o[...] = qk_norm(k_ref[...], kg_ref[...]).astype(BF16)
    v_o[...] = v_ref[...].astype(BF16)
    log_f = -_softplus(-(ff_ref[...] + fb_ref[...]))
    cum = _dot3r(_tri(tm, True, BF16), log_f) + carry_ref[...]
    carry_ref[...] = cum[tm - 1:tm, :]
    cum_t = cum.T[0:8, :]
    ct_o[0] = cum_t
    lane = lax.broadcasted_iota(jnp.int32, (1, blocks), 1)
    cend_o[0] = jnp.where(lane == j, cum_t[:, tm - 1:tm], cend_o[0])


def _fox_prep(seq_len, fox, ff, qg, kg, fb, ones, tm):
    n = fox.shape[0]
    blocks = seq_len // tm
    full = lambda a: pl.BlockSpec(a.shape, lambda i: (0,) * a.ndim)
    col = lambda j: pl.BlockSpec((tm, MIX_W), lambda i: (i, j))
    tile = pl.BlockSpec((tm, MIX_W), lambda i: (i, 0))
    return pl.pallas_call(
        functools.partial(_fox_prep_body, seq_len),
        grid=(n // tm,),
        in_specs=[col(0), col(1), col(2), pl.BlockSpec((tm, FF_PAD), lambda i: (i, 0)),
                  full(qg), full(kg), full(fb), full(ones)],
        out_specs=[tile, tile, tile, pl.BlockSpec((1, 8, tm), lambda i: (i, 0, 0)),
                   pl.BlockSpec((1, 8, blocks), lambda i: (i // blocks, 0, 0))],
        out_shape=[jax.ShapeDtypeStruct((n, MIX_W), BF16)] * 3
        + [jax.ShapeDtypeStruct((n // tm, 8, tm), F32),
           jax.ShapeDtypeStruct((n // seq_len, 8, blocks), F32)],
        scratch_shapes=[pltpu.VMEM((1, FF_PAD), F32)],
        compiler_params=_params("arbitrary"),
        name="fox_prep",
    )(fox, fox, fox, ff, qg, kg, fb, ones)


def _fox_attn_body(q_ref, k_ref, v_ref, ct_ref, cend_ref, qg_ref, kg_ref, o_ref):
    p = pl.program_id(1)
    i = pl.program_id(2)
    t = q_ref.shape[0]
    nq = cend_ref.shape[2]
    lane = lax.broadcasted_iota(jnp.int32, (1, PAIR), 1)
    left = lane < HEAD_DIM
    q = q_ref[...]
    q_h = (jnp.where(left, q, jnp.zeros_like(q)), jnp.where(left, jnp.zeros_like(q), q))
    sub = lax.broadcasted_iota(jnp.int32, (8, 1), 0)
    row_of_head = lambda x, h: jnp.sum(jnp.where(sub == 2 * p + h, x, 0.0), axis=0, keepdims=True)

    amax = lambda r: jnp.max(jnp.abs(r[...]), axis=1, keepdims=True)
    qk_bound = 8.0 * BF16_MARGIN * amax(qg_ref) * amax(kg_ref)
    blk = lax.broadcasted_iota(jnp.int32, (1, nq), 1)
    cend = cend_ref[0]
    c_ref = []
    needed = blk < 0
    for h in (0, 1):
        ce = row_of_head(cend, h)
        c_ref.append(jnp.sum(jnp.where(blk == i - 1, ce, 0.0), axis=1, keepdims=True))
        needed = needed | ((blk < i) & (c_ref[h] - ce > -(2.0 * qk_bound + UNDERFLOW_CUT)))
    first = i - jnp.sum(jnp.where(needed, 1, 0))

    def block(j, carry, diagonal):
        start = pl.multiple_of(j * t, t)
        k_j = k_ref[pl.ds(start, t), :]
        v_j = v_ref[pl.ds(start, t), :]
        ct = ct_ref[j]
        acc = carry[4]
        new = []
        upd = []
        for h in (0, 1):
            m, l = carry[2 * h], carry[2 * h + 1]
            z = _dot_nt(q_h[h], k_j) - (row_of_head(ct, h) - c_ref[h]) * LOG2E
            if diagonal:
                rr = lax.broadcasted_iota(jnp.int32, (t, t), 0)
                cc = lax.broadcasted_iota(jnp.int32, (t, t), 1)
                z = jnp.where(cc <= rr, z, -1e30)
            m_new = jnp.maximum(m, jnp.max(z, axis=1, keepdims=True))
            pr = jnp.exp2(z - m_new)
            alpha = jnp.exp2(m - m_new)
            new += [m_new, l * alpha + jnp.sum(pr, axis=1, keepdims=True)]
            upd.append(acc * alpha + jnp.dot(pr.astype(BF16), v_j, preferred_element_type=F32))
        return (*new, jnp.where(left, upd[0], upd[1]))

    neg = jnp.full((t, 1), -1e30, F32)
    zero = jnp.zeros((t, 1), F32)
    carry = block(i, (neg, zero, neg, zero, jnp.zeros((t, PAIR), F32)), True)
    carry = lax.fori_loop(first, i, lambda j, c: block(j, c, False), carry)
    o_ref[...] = jnp.where(left, carry[4] / carry[1], carry[4] / carry[3])


def _fox_attn(seq_len, q, k, v, ct, cend, qg, kg, t):
    n = q.shape[0]
    nq = seq_len // t
    full = lambda a: pl.BlockSpec(a.shape, lambda bi, p, i: (0,) * a.ndim)
    return pl.pallas_call(
        _fox_attn_body,
        grid=(n // seq_len, N_PAIRS, nq),
        in_specs=[pl.BlockSpec((t, PAIR), lambda bi, p, i: (bi * nq + i, p)),
                  pl.BlockSpec((seq_len, PAIR), lambda bi, p, i: (bi, p)),
                  pl.BlockSpec((seq_len, PAIR), lambda bi, p, i: (bi, p)),
                  pl.BlockSpec((nq, 8, t), lambda bi, p, i: (bi, 0, 0)),
                  pl.BlockSpec((1, 8, nq), lambda bi, p, i: (bi, 0, 0)),
                  full(qg), full(kg)],
        out_specs=pl.BlockSpec((t, PAIR), lambda bi, p, i: (bi * nq + i, p)),
        out_shape=jax.ShapeDtypeStruct((n, MIX_W), F32),
        compiler_params=_params("parallel", "parallel", "arbitrary"),
        name="fox_attn",
    )(q, k, v, ct, cend, qg, kg)


def _mix_out_body(seq_len, x_ref, ys_ref, g_ref, bonus_ref, at_ref, fg_ref, pool_ref, halo_ref,
                  ln_ref, ones_ref, pw_ref, ps_ref, wo_ref, pg_ref, o_ref):
    tm = x_ref.shape[0]
    row0 = lax.rem(pl.program_id(0) * tm, seq_len)
    ones = ones_ref[...]

    y = ys_ref[...]
    d = y - _dot3(y, ones) * (1.0 / HEAD_DIM)
    var = _dot3(d * d, ones) * (1.0 / HEAD_DIM)
    y_rwkv = (d * lax.rsqrt(var + GN_EPS) * ln_ref[0:1, :] + ln_ref[1:2, :] + bonus_ref[...]) * g_ref[...]

    y_fox = at_ref[...] * _sigmoid(fg_ref[...])

    pin = pool_ref[...]
    halo = jnp.where(row0 == 0, 0.0, halo_ref[...])
    s = jnp.concatenate([halo, pin], axis=0)
    sums = []
    for shift in (1, 2, 4, 8):
        s = s + pltpu.roll(s, shift, axis=0)
        sums.append(s[POOL_HALO:, :])
    grp = lax.broadcasted_iota(jnp.int32, (1, POOL_W), 1) // HEAD_DIM
    win_sum = jnp.where(grp == 0, sums[0], jnp.where(grp == 1, sums[1], jnp.where(grp == 2, sums[2], sums[3])))
    win = jnp.where(grp == 0, 2.0, jnp.where(grp == 1, 4.0, jnp.where(grp == 2, 8.0, 16.0)))
    pos = (row0 + lax.broadcasted_iota(jnp.int32, (tm, 1), 0) + 1).astype(F32)
    u = win_sum / jnp.minimum(pos, win) - pin
    y_pool = _dot(u, pw_ref[...]) * ps_ref[...]

    mixed = (_dot(y_rwkv, wo_ref[0:384, :]) + _dot(y_fox, wo_ref[384:768, :])
             + _dot(y_pool, wo_ref[768:1024, :]))
    o_ref[...] = x_ref[...] + _rms(mixed, pg_ref[...])


def _mix_out(seq_len, x, y_scan, g, bonus, attn, fox, pool_in, ln, ones, pw, ps, wo, pg, tm=256):
    n = x.shape[0]
    full = lambda a: pl.BlockSpec(a.shape, lambda i: (0,) * a.ndim)
    tile = pl.BlockSpec((tm, MIX_W), lambda i: (i, 0))
    return pl.pallas_call(
        functools.partial(_mix_out_body, seq_len),
        grid=(n // tm,),
        in_specs=[pl.BlockSpec((tm, D_MODEL), lambda i: (i, 0)), tile, tile, tile, tile,
                  pl.BlockSpec((tm, MIX_W), lambda i: (i, 3)),
                  pl.BlockSpec((tm, POOL_W), lambda i: (i, 0)),
                  pl.BlockSpec((POOL_HALO, POOL_W),
                               lambda i: (jnp.maximum(i * (tm // POOL_HALO) - 1, 0), 0)),
                  full(ln), full(ones), full(pw), full(ps), full(wo), full(pg)],
        out_specs=pl.BlockSpec((tm, D_MODEL), lambda i: (i, 0)),
        out_shape=jax.ShapeDtypeStruct((n, D_MODEL), F32),
        compiler_params=_params("parallel"),
        name="mix_out",
    )(x, y_scan, g, bonus, attn, fox, pool_in, pool_in, ln, ones, pw, ps, wo, pg)


def _ffn_body(x_ref, g1_ref, wgu_ref, wd_ref, g2_ref, o_ref):
    x = x_ref[...]
    h = _rms(x, g1_ref[...]).astype(BF16)
    acc = jnp.zeros(x.shape, F32)
    for c in range(FFN_HIDDEN // FFN_CHUNK):
        c0 = c * FFN_CHUNK
        gate = jnp.dot(h, wgu_ref[:, c0:c0 + FFN_CHUNK], preferred_element_type=F32)
        up = jnp.dot(h, wgu_ref[:, FFN_HIDDEN + c0:FFN_HIDDEN + c0 + FFN_CHUNK],
                     preferred_element_type=F32)
        act = (gate * _sigmoid(gate) * up).astype(BF16)
        acc = acc + jnp.dot(act, wd_ref[c0:c0 + FFN_CHUNK, :], preferred_element_type=F32)
    o_ref[...] = x + _rms(acc, g2_ref[...])


def _ffn(x, g1, wgu, wd, g2, tm=256):
    n = x.shape[0]
    full = lambda a: pl.BlockSpec(a.shape, lambda i: (0,) * a.ndim)
    tile = pl.BlockSpec((tm, D_MODEL), lambda i: (i, 0))
    return pl.pallas_call(
        _ffn_body,
        grid=(n // tm,),
        in_specs=[tile, full(g1), full(wgu), full(wd), full(g2)],
        out_specs=tile,
        out_shape=jax.ShapeDtypeStruct((n, D_MODEL), F32),
        compiler_params=_params("parallel"),
        name="ffn",
    )(x, g1, wgu, wd, g2)


def _pad_rows(a, rows, at=0):
    out = jnp.zeros((rows, a.shape[1]), a.dtype)
    return lax.dynamic_update_slice(out, a, (at, 0))


def _block_diag_ones():
    h = jnp.arange(MIX_W) // HEAD_DIM
    return (h[:, None] == h[None, :]).astype(BF16)


def kernel(x, mix_pre_g, mix_post_g, ffn_pre_g, ffn_post_g, w_in, w_out, rwkv_mu, rwkv_w0, rwkv_w2,
           rwkv_a0, rwkv_a2, rwkv_g2, rwkv_v0, rwkv_v1, rwkv_v2, rwkv_k_k, rwkv_k_a, rwkv_r_k,
           rwkv_ln_w, rwkv_ln_b, fox_q_g, fox_k_g, fox_f_b, pool_w, pool_scale, ffn_w_gu, ffn_w_down):
    batch, seq_len, _ = x.shape
    n = batch * seq_len
    depth = w_in.shape[0]
    ones = _block_diag_ones()
    row = lambda a: a.reshape(1, -1).astype(F32)
    xf = x.reshape(n, D_MODEL)
    v_first = None
    for l in range(depth):
        w = w_in[l]
        z = lambda c: jnp.zeros((D_MODEL, c), w.dtype)
        w_arr = jnp.concatenate([w[:, :RWKV_IN], z(RWKV_PAD - RWKV_IN), w[:, RWKV_IN:RWKV_IN + FOX_IN],
                                 w[:, 2976:2982], z(FF_PAD - N_HEADS), w[:, 2982:]], axis=1).astype(BF16)
        feats, fox, ff, pool_in = _inproj(xf, row(mix_pre_g[l]), w_arr)

        mu = jnp.pad(rwkv_mu[l], (0, RWKV_PAD - RWKV_IN)).reshape(1, RWKV_PAD)
        v0 = rwkv_v0[l - 1] if l > 0 else jnp.zeros((MIX_W,), F32)
        vecs = jnp.stack([rwkv_w0[l], rwkv_a0[l], rwkv_k_k[l], rwkv_k_a[l], rwkv_r_k[l].reshape(-1), v0,
                          jnp.zeros((MIX_W,), F32), jnp.zeros((MIX_W,), F32)])
        w2p = _pad_rows(rwkv_w2[l], 128, 0).astype(BF16)
        a2p = _pad_rows(rwkv_a2[l], 128, 64).astype(BF16)
        g2p = _pad_rows(rwkv_g2[l], 256, 0).astype(BF16)
        vres = None
        if l > 0:
            v1p = jnp.pad(rwkv_v1[l - 1], ((0, 0), (0, 128 - 32))).astype(BF16)
            v2p = _pad_rows(rwkv_v2[l - 1], 128, 0).astype(BF16)
            vres = (v_first, v1p, v2p)
        r, k, v, lw, kk, b, g, bonus = _rwkv_prep(seq_len, feats, mu, vecs, w2p, a2p, g2p, ones, vres)
        if l == 0:
            v_first = v
        y_scan = _rwkv_scan(seq_len, r, lw, k, v, kk, b)

        tile6 = lambda a: jnp.tile(a, N_HEADS).reshape(1, MIX_W)
        fb = jnp.pad(fox_f_b[l], (0, FF_PAD - N_HEADS)).reshape(1, FF_PAD)
        qg, kg = tile6(fox_q_g[l]), tile6(fox_k_g[l])
        q_n, k_n, v_b, ct, cend = _fox_prep(seq_len, fox, ff, qg, kg, fb, ones, FOX_BLOCK)
        attn = _fox_attn(seq_len, q_n, k_n, v_b, ct, cend, qg, kg, FOX_BLOCK)

        ln = jnp.stack([rwkv_ln_w[l], rwkv_ln_b[l]] + [jnp.zeros((MIX_W,), F32)] * 6)
        pw = jax.scipy.linalg.block_diag(*[pool_w[l, gi] for gi in range(4)]).astype(BF16)
        xf = _mix_out(seq_len, xf, y_scan, g, bonus, attn, fox, pool_in, ln, ones, pw,
                      row(pool_scale[l]), w_out[l].astype(BF16), row(mix_post_g[l]))
        xf = _ffn(xf, row(ffn_pre_g[l]), ffn_w_gu[l].astype(BF16), ffn_w_down[l].astype(BF16),
                  row(ffn_post_g[l]))
    return xf.reshape(batch, seq_len, D_MODEL)
```

```python
import functools

import jax
import jax.numpy as jnp
from jax import lax
from jax.experimental import pallas as pl
from jax.experimental.pallas import tpu as pltpu

F32 = jnp.float32
BF16 = jnp.bfloat16

D_MODEL = 1024
HEAD_DIM = 64
PAIR = 2 * HEAD_DIM
N_HEADS = 6
N_PAIRS = N_HEADS // 2
MIX_W = N_HEADS * HEAD_DIM
POOL_W = 256
POOL_HALO = 16
FFN_HIDDEN = 2816
FFN_CHUNK = 256
RMS_EPS = 1e-6
GN_EPS = HEAD_DIM * 1e-5

RWKV_IN = 1440
RWKV_PAD = 1536
FOX_IN = 4 * MIX_W
FF_PAD = 128
IN_SPLITS = ((0, 1536), (1536, 3072), (3072, 3200), (3200, 3456))
IN_TOTAL = 3456

LOG2E = 1.4426950408889634
FOX_BLOCK = 256
FOX_WIDE = 4
UNDERFLOW_CUT = 110.0
BF16_MARGIN = 1.02
SCAN_CHUNK = 64
VMEM_LIMIT = 56 * 1024 * 1024


def _params(*sem):
    return pltpu.CompilerParams(dimension_semantics=sem, vmem_limit_bytes=VMEM_LIMIT)


def _dot(a, b):
    return jnp.dot(a.astype(BF16), b.astype(BF16), preferred_element_type=F32)


def _dot_nt(a, b):
    return lax.dot_general(a.astype(BF16), b.astype(BF16), (((1,), (1,)), ((), ())),
                           preferred_element_type=F32)


def _split3(x):
    hi = x.astype(BF16)
    r1 = x - hi.astype(F32)
    mid = r1.astype(BF16)
    lo = (r1 - mid.astype(F32)).astype(BF16)
    return hi, mid, lo


def _dot3(x, w):
    hi, mid, lo = _split3(x)
    d = lambda t: jnp.dot(t, w, preferred_element_type=F32)
    return d(hi) + d(mid) + d(lo)


def _dot3r(w, x):
    hi, mid, lo = _split3(x)
    d = lambda t: jnp.dot(w, t, preferred_element_type=F32)
    return d(hi) + d(mid) + d(lo)


def _rms(x, g):
    ms = jnp.mean(x * x, axis=-1, keepdims=True)
    return x * lax.rsqrt(ms + RMS_EPS) * g


def _softplus(z):
    return jnp.maximum(z, 0.0) + jnp.log1p(jnp.exp(-jnp.abs(z)))


def _sigmoid(z):
    return 1.0 / (1.0 + jnp.exp(-z))


def _tri(n, inclusive, dtype):
    r = lax.broadcasted_iota(jnp.int32, (n, n), 0)
    c = lax.broadcasted_iota(jnp.int32, (n, n), 1)
    keep = (r >= c) if inclusive else (r > c)
    return jnp.where(keep, 1.0, 0.0).astype(dtype)


def _inproj_body(x_ref, g_ref, w_ref, o_rwkv, o_fox, o_ff, o_pool):
    h = _rms(x_ref[...], g_ref[...]).astype(BF16)
    for o_ref, (c0, c1) in zip((o_rwkv, o_fox, o_ff, o_pool), IN_SPLITS):
        o_ref[...] = jnp.dot(h, w_ref[:, c0:c1], preferred_element_type=F32)


def _inproj(x, g, w, tm=256):
    n = x.shape[0]
    widths = [c1 - c0 for c0, c1 in IN_SPLITS]
    return pl.pallas_call(
        _inproj_body,
        grid=(n // tm,),
        in_specs=[pl.BlockSpec((tm, D_MODEL), lambda i: (i, 0)),
                  pl.BlockSpec((1, D_MODEL), lambda i: (0, 0)),
                  pl.BlockSpec((D_MODEL, IN_TOTAL), lambda i: (0, 0))],
        out_specs=[pl.BlockSpec((tm, w_), lambda i: (i, 0)) for w_ in widths],
        out_shape=[jax.ShapeDtypeStruct((n, w_), F32) for w_ in widths],
        compiler_params=_params("parallel"),
        name="inproj",
    )(x, g, w)


def _rwkv_prep_body(seq_len, has_vres, *refs):
    if has_vres:
        (f_ref, halo_ref, mu_ref, vec_ref, w2_ref, a2_ref, g2_ref, ones_ref, vf_ref, v1_ref, v2_ref,
         r_o, k_o, v_o, lw_o, kk_o, b_o, g_o, bonus_o) = refs
    else:
        (f_ref, halo_ref, mu_ref, vec_ref, w2_ref, a2_ref, g2_ref, ones_ref,
         r_o, k_o, v_o, lw_o, kk_o, b_o, g_o, bonus_o) = refs
    tm = f_ref.shape[0]
    at_start = lax.rem(pl.program_id(0) * tm, seq_len) == 0
    f = f_ref[...]
    halo = jnp.where(at_start, 0.0, halo_ref[7:8, :])
    row = lax.broadcasted_iota(jnp.int32, (tm, 1), 0)
    prev = jnp.where(row == 0, halo, pltpu.roll(f, 1, axis=0))
    f = f + (prev - f) * mu_ref[...]
    r, k, v = f[:, 0:384], f[:, 384:768], f[:, 768:1152]
    wa = f[:, 1152:1280]
    gd = f[:, 1280:1536]
    w0, a0, k_k, k_a, r_k, v0 = (vec_ref[j:j + 1, :] for j in range(6))

    w_log = -_softplus(-(w0 + _dot(jnp.tanh(wa), w2_ref[...]))) - 0.5
    lw_o[...] = -jnp.exp(w_log)
    a = _sigmoid(a0 + _dot(wa, a2_ref[...]))
    g_o[...] = _dot(_sigmoid(gd), g2_ref[...])
    if has_vres:
        gate = _sigmoid(v0 + _dot(_dot(v, v1_ref[...]), v2_ref[...]))
        v = v + (vf_ref[...] - v) * gate
    kk = k * k_k
    ss = _dot3(kk * kk, ones_ref[...])
    kk = kk / jnp.maximum(jnp.sqrt(ss), 1e-12)
    k = k * (1.0 + (a - 1.0) * k_a)
    r_o[...] = r
    k_o[...] = k
    v_o[...] = v
    kk_o[...] = kk
    b_o[...] = kk * a
    bonus_o[...] = _dot3(r * k * r_k, ones_ref[...]) * v


def _rwkv_prep(seq_len, feats, mu, vecs, w2p, a2p, g2p, ones, vres, tm=256):
    n = feats.shape[0]
    full = lambda a: pl.BlockSpec(a.shape, lambda i: (0,) * a.ndim)
    tile = pl.BlockSpec((tm, MIX_W), lambda i: (i, 0))
    in_specs = [pl.BlockSpec((tm, RWKV_PAD), lambda i: (i, 0)),
                pl.BlockSpec((8, RWKV_PAD), lambda i: (jnp.maximum(i * (tm // 8) - 1, 0), 0)),
                full(mu), full(vecs), full(w2p), full(a2p), full(g2p), full(ones)]
    args = [feats, feats, mu, vecs, w2p, a2p, g2p, ones]
    if vres is not None:
        v_first, v1p, v2p = vres
        in_specs += [tile, full(v1p), full(v2p)]
        args += [v_first, v1p, v2p]
    return pl.pallas_call(
        functools.partial(_rwkv_prep_body, seq_len, vres is not None),
        grid=(n // tm,),
        in_specs=in_specs,
        out_specs=[tile] * 8,
        out_shape=[jax.ShapeDtypeStruct((n, MIX_W), F32)] * 8,
        compiler_params=_params("parallel"),
        name="rwkv_prep",
    )(*args)


def _scan_body(chunks_per_step, r_ref, lw_ref, k_ref, v_ref, kk_ref, b_ref, y_ref, h_ref):
    c_len = SCAN_CHUNK
    n_batch = r_ref.shape[0]

    @pl.when(pl.program_id(0) == 0)
    def _():
        h_ref[...] = jnp.zeros_like(h_ref)

    lane = lax.broadcasted_iota(jnp.int32, (1, PAIR), 1)
    left = lane < HEAD_DIM
    ri = lax.broadcasted_iota(jnp.int32, (PAIR, PAIR), 0)
    ci = lax.broadcasted_iota(jnp.int32, (PAIR, PAIR), 1)
    strict, incl, diag = ri > ci, ri >= ci, ri == ci
    tri = _tri(c_len, True, BF16)
    eye = jnp.where(diag, 1.0, 0.0)

    def stack(x):
        return jnp.concatenate([jnp.where(left, x, 0.0), jnp.where(left, 0.0, x)], axis=0)

    seqs = [(bi, p) for bi in range(n_batch) for p in range(N_PAIRS)]
    items = [(bi, p, c) for (bi, p) in seqs for c in range(chunks_per_step)]
    window = lambda bi, p, c: (bi, slice(c * c_len, (c + 1) * c_len), slice(p * PAIR, (p + 1) * PAIR))
    load = lambda ref: [ref[window(*it)] for it in items]
    each = lambda f, *cols: [f(*xs) for xs in zip(*cols)]

    lw = load(lw_ref)
    cum = each(lambda x: _dot3r(tri, x), lw)
    cum_end = each(lambda x: x[c_len - 1:c_len, :], cum)
    e_neg = each(lambda x: jnp.exp(-x), cum)
    e_tail = each(lambda x, xe: jnp.exp(xe - x), cum, cum_end)
    kk, b, k = load(kk_ref), load(b_ref), load(k_ref)
    l_a = each(lambda x, c_, w_: stack(-x * jnp.exp(c_ - w_)), kk, cum, lw)
    l_r = each(lambda x, c_: stack(x * jnp.exp(c_)), load(r_ref), cum)
    r_b = each(lambda x, e: stack(x * e), b, e_neg)
    r_k = each(lambda x, e: stack(x * e), k, e_neg)
    rb_t = each(lambda x, e: stack(x * e).T, b, e_tail)
    rk_t = each(lambda x, e: stack(x * e).T, k, e_tail)
    v_s = each(stack, load(v_ref))

    a_ab = each(lambda x, y: jnp.where(strict, _dot_nt(x, y), 0.0), l_a, r_b)
    a_ak = each(lambda x, y: jnp.where(strict, _dot_nt(x, y), 0.0), l_a, r_k)
    a_rb = each(lambda x, y: jnp.where(incl, _dot_nt(x, y), 0.0), l_r, r_b)
    a_rk = each(lambda x, y: jnp.where(incl, _dot_nt(x, y), 0.0), l_r, r_k)

    x = a_ab
    t_inv = each(lambda a: eye + a, x)
    for _ in range(5):
        x = each(lambda a: _dot(a, a), x)
        t_inv = each(lambda t, a: t + _dot(t, a), t_inv, x)

    t_a = each(_dot, t_inv, l_a)
    w_ = each(lambda t, a, v: _dot(t, _dot(a, v)), t_inv, a_ak, v_s)
    r_q = each(lambda l, a, t: l + _dot(a, t), l_r, a_rb, t_a)
    y_0 = each(lambda a, w, a2, v: _dot(a, w) + _dot(a2, v), a_rb, w_, a_rk, v_s)
    m_ = each(lambda ce, bt, t: jnp.where(diag, jnp.exp(ce), 0.0) + _dot(bt, t), cum_end, rb_t, t_a)
    n_ = each(lambda bt, w, kt, v: _dot(bt, w) + _dot(kt, v), rb_t, w_, rk_t, v_s)

    for s, (bi, p) in enumerate(seqs):
        h = h_ref[s]
        for c in range(chunks_per_step):
            idx = s * chunks_per_step + c
            y_st = _dot(r_q[idx], h) + y_0[idx]
            y_ref[window(bi, p, c)] = y_st[:c_len] + y_st[c_len:]
            h = _dot(m_[idx], h) + n_[idx]
        h_ref[s] = h


def _rwkv_scan(seq_len, r, lw, k, v, kk, b, chunks_per_step=2):
    n = r.shape[0]
    n_batch = n // seq_len
    rows = chunks_per_step * SCAN_CHUNK
    spec = pl.BlockSpec((n_batch, rows, MIX_W), lambda j: (0, j, 0))
    args = [a.reshape(n_batch, seq_len, MIX_W) for a in (r, lw, k, v, kk, b)]
    y = pl.pallas_call(
        functools.partial(_scan_body, chunks_per_step),
        grid=(seq_len // rows,),
        in_specs=[spec] * 6,
        out_specs=spec,
        out_shape=jax.ShapeDtypeStruct((n_batch, seq_len, MIX_W), F32),
        scratch_shapes=[pltpu.VMEM((n_batch * N_PAIRS, PAIR, PAIR), F32)],
        compiler_params=_params("arbitrary"),
        name="rwkv_scan",
    )(*args)
    return y.reshape(n, MIX_W)


def _fox_prep_body(seq_len, q_ref, k_ref, v_ref, ff_ref, qg_ref, kg_ref, fb_ref, ones_ref,
                   q_o, k_o, v_o, ct_o, cend_o, carry_ref):
    tm = q_ref.shape[0]
    blocks = seq_len // tm
    j = lax.rem(pl.program_id(0), blocks)

    @pl.when(j == 0)
    def _():
        carry_ref[...] = jnp.zeros_like(carry_ref)
        cend_o[...] = jnp.zeros_like(cend_o)

    def qk_norm(x, g):
        ms = _dot3(x * x, ones_ref[...]) * (1.0 / HEAD_DIM)
        return x * lax.rsqrt(ms + RMS_EPS) * g

    q_o[...] = (qk_norm(q_ref[...], qg_ref[...]) * (LOG2E * HEAD_DIM ** -0.5)).astype(BF16)
    k_o[...] = qk_norm(k_ref[...], kg_ref[...]).astype(BF16)
    v_o[...] = v_ref[...].astype(BF16)
    log_f = -_softplus(-(ff_ref[...] + fb_ref[...]))
    cum = _dot3r(_tri(tm, True, BF16), log_f) + carry_ref[...]
    carry_ref[...] = cum[tm - 1:tm, :]
    cum_t = cum.T[0:8, :]
    ct_o[0] = cum_t
    lane = lax.broadcasted_iota(jnp.int32, (1, blocks), 1)
    cend_o[0] = jnp.where(lane == j, cum_t[:, tm - 1:tm], cend_o[0])


def _fox_prep(seq_len, fox, ff, qg, kg, fb, ones, tm):
    n = fox.shape[0]
    blocks = seq_len // tm
    full = lambda a: pl.BlockSpec(a.shape, lambda i: (0,) * a.ndim)
    col = lambda j: pl.BlockSpec((tm, MIX_W), lambda i: (i, j))
    tile = pl.BlockSpec((tm, MIX_W), lambda i: (i, 0))
    return pl.pallas_call(
        functools.partial(_fox_prep_body, seq_len),
        grid=(n // tm,),
        in_specs=[col(0), col(1), col(2), pl.BlockSpec((tm, FF_PAD), lambda i: (i, 0)),
                  full(qg), full(kg), full(fb), full(ones)],
        out_specs=[tile, tile, tile, pl.BlockSpec((1, 8, tm), lambda i: (i, 0, 0)),
                   pl.BlockSpec((1, 8, blocks), lambda i: (i // blocks, 0, 0))],
        out_shape=[jax.ShapeDtypeStruct((n, MIX_W), BF16)] * 3
        + [jax.ShapeDtypeStruct((n // tm, 8, tm), F32),
           jax.ShapeDtypeStruct((n // seq_len, 8, blocks), F32)],
        scratch_shapes=[pltpu.VMEM((1, FF_PAD), F32)],
        compiler_params=_params("arbitrary"),
        name="fox_prep",
    )(fox, fox, fox, ff, qg, kg, fb, ones)


def _fox_attn_body(q_ref, k_ref, v_ref, ct_ref, cend_ref, qg_ref, kg_ref, o_ref):
    p = pl.program_id(1)
    i = pl.program_id(2)
    t = q_ref.shape[0]
    nq = cend_ref.shape[2]
    lane = lax.broadcasted_iota(jnp.int32, (1, PAIR), 1)
    left = lane < HEAD_DIM
    q = q_ref[...]
    q_h = (jnp.where(left, q, jnp.zeros_like(q)), jnp.where(left, jnp.zeros_like(q), q))
    sub = lax.broadcasted_iota(jnp.int32, (8, 1), 0)
    row_of_head = lambda x, h: jnp.sum(jnp.where(sub == 2 * p + h, x, 0.0), axis=0, keepdims=True)

    amax = lambda r: jnp.max(jnp.abs(r[...]), axis=1, keepdims=True)
    qk_bound = 8.0 * BF16_MARGIN * amax(qg_ref) * amax(kg_ref)
    blk = lax.broadcasted_iota(jnp.int32, (1, nq), 1)
    cend = cend_ref[0]
    c_ref = []
    needed = blk < 0
    for h in (0, 1):
        ce = row_of_head(cend, h)
        c_ref.append(jnp.sum(jnp.where(blk == i - 1, ce, 0.0), axis=1, keepdims=True))
        needed = needed | ((blk < i) & (c_ref[h] - ce > -(2.0 * qk_bound + UNDERFLOW_CUT)))
    first = i - jnp.sum(jnp.where(needed, 1, 0))

    def block(j, width, carry, diagonal):
        start = pl.multiple_of(j * t, t)
        k_j = k_ref[pl.ds(start, width * t), :]
        v_j = v_ref[pl.ds(start, width * t), :]
        cts = [ct_ref[j + u] for u in range(width)]
        acc = carry[4]
        new = []
        upd = []
        for h in (0, 1):
            m, l = carry[2 * h], carry[2 * h + 1]
            c_k = jnp.concatenate([row_of_head(ct, h) for ct in cts], axis=1)
            z = _dot_nt(q_h[h], k_j) - (c_k - c_ref[h]) * LOG2E
            if diagonal:
                rr = lax.broadcasted_iota(jnp.int32, (t, t), 0)
                cc = lax.broadcasted_iota(jnp.int32, (t, t), 1)
                z = jnp.where(cc <= rr, z, -1e30)
            m_new = jnp.maximum(m, jnp.max(z, axis=1, keepdims=True))
            pr = jnp.exp2(z - m_new)
            alpha = jnp.exp2(m - m_new)
            new += [m_new, l * alpha + jnp.sum(pr, axis=1, keepdims=True)]
            upd.append(acc * alpha + jnp.dot(pr.astype(BF16), v_j, preferred_element_type=F32))
        return (*new, jnp.where(left, upd[0], upd[1]))

    neg = jnp.full((t, 1), -1e30, F32)
    zero = jnp.zeros((t, 1), F32)
    carry = block(i, 1, (neg, zero, neg, zero, jnp.zeros((t, PAIR), F32)), True)
    wide = jnp.minimum((i - first + FOX_WIDE - 1) // FOX_WIDE, i // FOX_WIDE)
    lo = i - wide * FOX_WIDE
    carry = lax.fori_loop(0, wide, lambda g, c: block(lo + g * FOX_WIDE, FOX_WIDE, c, False), carry)
    carry = lax.fori_loop(first, lo, lambda j, c: block(j, 1, c, False), carry)
    o_ref[...] = jnp.where(left, carry[4] / carry[1], carry[4] / carry[3])


def _fox_attn(seq_len, q, k, v, ct, cend, qg, kg, t):
    n = q.shape[0]
    nq = seq_len // t
    full = lambda a: pl.BlockSpec(a.shape, lambda bi, p, i: (0,) * a.ndim)
    return pl.pallas_call(
        _fox_attn_body,
        grid=(n // seq_len, N_PAIRS, nq),
        in_specs=[pl.BlockSpec((t, PAIR), lambda bi, p, i: (bi * nq + i, p)),
                  pl.BlockSpec((seq_len, PAIR), lambda bi, p, i: (bi, p)),
                  pl.BlockSpec((seq_len, PAIR), lambda bi, p, i: (bi, p)),
                  pl.BlockSpec((nq, 8, t), lambda bi, p, i: (bi, 0, 0)),
                  pl.BlockSpec((1, 8, nq), lambda bi, p, i: (bi, 0, 0)),
                  full(qg), full(kg)],
        out_specs=pl.BlockSpec((t, PAIR), lambda bi, p, i: (bi * nq + i, p)),
        out_shape=jax.ShapeDtypeStruct((n, MIX_W), F32),
        compiler_params=_params("parallel", "parallel", "arbitrary"),
        name="fox_attn",
    )(q, k, v, ct, cend, qg, kg)


def _mix_out_body(seq_len, x_ref, ys_ref, g_ref, bonus_ref, at_ref, fg_ref, pool_ref, halo_ref,
                  ln_ref, ones_ref, pw_ref, ps_ref, wo_ref, pg_ref, o_ref):
    tm = x_ref.shape[0]
    row0 = lax.rem(pl.program_id(0) * tm, seq_len)
    ones = ones_ref[...]

    y = ys_ref[...]
    d = y - _dot3(y, ones) * (1.0 / HEAD_DIM)
    var = _dot3(d * d, ones) * (1.0 / HEAD_DIM)
    y_rwkv = (d * lax.rsqrt(var + GN_EPS) * ln_ref[0:1, :] + ln_ref[1:2, :] + bonus_ref[...]) * g_ref[...]

    y_fox = at_ref[...] * _sigmoid(fg_ref[...])

    pin = pool_ref[...]
    halo = jnp.where(row0 == 0, 0.0, halo_ref[...])
    s = jnp.concatenate([halo, pin], axis=0)
    sums = []
    for shift in (1, 2, 4, 8):
        s = s + pltpu.roll(s, shift, axis=0)
        sums.append(s[POOL_HALO:, :])
    grp = lax.broadcasted_iota(jnp.int32, (1, POOL_W), 1) // HEAD_DIM
    win_sum = jnp.where(grp == 0, sums[0], jnp.where(grp == 1, sums[1], jnp.where(grp == 2, sums[2], sums[3])))
    win = jnp.where(grp == 0, 2.0, jnp.where(grp == 1, 4.0, jnp.where(grp == 2, 8.0, 16.0)))
    pos = (row0 + lax.broadcasted_iota(jnp.int32, (tm, 1), 0) + 1).astype(F32)
    u = win_sum / jnp.minimum(pos, win) - pin
    y_pool = _dot(u, pw_ref[...]) * ps_ref[...]

    mixed = (_dot(y_rwkv, wo_ref[0:384, :]) + _dot(y_fox, wo_ref[384:768, :])
             + _dot(y_pool, wo_ref[768:1024, :]))
    o_ref[...] = x_ref[...] + _rms(mixed, pg_ref[...])


def _mix_out(seq_len, x, y_scan, g, bonus, attn, fox, pool_in, ln, ones, pw, ps, wo, pg, tm=256):
    n = x.shape[0]
    full = lambda a: pl.BlockSpec(a.shape, lambda i: (0,) * a.ndim)
    tile = pl.BlockSpec((tm, MIX_W), lambda i: (i, 0))
    return pl.pallas_call(
        functools.partial(_mix_out_body, seq_len),
        grid=(n // tm,),
        in_specs=[pl.BlockSpec((tm, D_MODEL), lambda i: (i, 0)), tile, tile, tile, tile,
                  pl.BlockSpec((tm, MIX_W), lambda i: (i, 3)),
                  pl.BlockSpec((tm, POOL_W), lambda i: (i, 0)),
                  pl.BlockSpec((POOL_HALO, POOL_W),
                               lambda i: (jnp.maximum(i * (tm // POOL_HALO) - 1, 0), 0)),
                  full(ln), full(ones), full(pw), full(ps), full(wo), full(pg)],
        out_specs=pl.BlockSpec((tm, D_MODEL), lambda i: (i, 0)),
        out_shape=jax.ShapeDtypeStruct((n, D_MODEL), F32),
        compiler_params=_params("parallel"),
        name="mix_out",
    )(x, y_scan, g, bonus, attn, fox, pool_in, pool_in, ln, ones, pw, ps, wo, pg)


def _ffn_body(x_ref, g1_ref, wgu_ref, wd_ref, g2_ref, o_ref):
    x = x_ref[...]
    h = _rms(x, g1_ref[...]).astype(BF16)
    acc = jnp.zeros(x.shape, F32)
    for c in range(FFN_HIDDEN // FFN_CHUNK):
        c0 = c * FFN_CHUNK
        gate = jnp.dot(h, wgu_ref[:, c0:c0 + FFN_CHUNK], preferred_element_type=F32)
        up = jnp.dot(h, wgu_ref[:, FFN_HIDDEN + c0:FFN_HIDDEN + c0 + FFN_CHUNK],
                     preferred_element_type=F32)
        act = (gate * _sigmoid(gate) * up).astype(BF16)
        acc = acc + jnp.dot(act, wd_ref[c0:c0 + FFN_CHUNK, :], preferred_element_type=F32)
    o_ref[...] = x + _rms(acc, g2_ref[...])


def _ffn(x, g1, wgu, wd, g2, tm=256):
    n = x.shape[0]
    full = lambda a: pl.BlockSpec(a.shape, lambda i: (0,) * a.ndim)
    tile = pl.BlockSpec((tm, D_MODEL), lambda i: (i, 0))
    return pl.pallas_call(
        _ffn_body,
        grid=(n // tm,),
        in_specs=[tile, full(g1), full(wgu), full(wd), full(g2)],
        out_specs=tile,
        out_shape=jax.ShapeDtypeStruct((n, D_MODEL), F32),
        compiler_params=_params("parallel"),
        name="ffn",
    )(x, g1, wgu, wd, g2)


def _pad_rows(a, rows, at=0):
    out = jnp.zeros((rows, a.shape[1]), a.dtype)
    return lax.dynamic_update_slice(out, a, (at, 0))


def _block_diag_ones():
    h = jnp.arange(MIX_W) // HEAD_DIM
    return (h[:, None] == h[None, :]).astype(BF16)


def kernel(x, mix_pre_g, mix_post_g, ffn_pre_g, ffn_post_g, w_in, w_out, rwkv_mu, rwkv_w0, rwkv_w2,
           rwkv_a0, rwkv_a2, rwkv_g2, rwkv_v0, rwkv_v1, rwkv_v2, rwkv_k_k, rwkv_k_a, rwkv_r_k,
           rwkv_ln_w, rwkv_ln_b, fox_q_g, fox_k_g, fox_f_b, pool_w, pool_scale, ffn_w_gu, ffn_w_down):
    batch, seq_len, _ = x.shape
    n = batch * seq_len
    depth = w_in.shape[0]
    ones = _block_diag_ones()
    row = lambda a: a.reshape(1, -1).astype(F32)
    xf = x.reshape(n, D_MODEL)
    v_first = None
    for l in range(depth):
        w = w_in[l]
        z = lambda c: jnp.zeros((D_MODEL, c), w.dtype)
        w_arr = jnp.concatenate([w[:, :RWKV_IN], z(RWKV_PAD - RWKV_IN), w[:, RWKV_IN:RWKV_IN + FOX_IN],
                                 w[:, 2976:2982], z(FF_PAD - N_HEADS), w[:, 2982:]], axis=1).astype(BF16)
        feats, fox, ff, pool_in = _inproj(xf, row(mix_pre_g[l]), w_arr)

        mu = jnp.pad(rwkv_mu[l], (0, RWKV_PAD - RWKV_IN)).reshape(1, RWKV_PAD)
        v0 = rwkv_v0[l - 1] if l > 0 else jnp.zeros((MIX_W,), F32)
        vecs = jnp.stack([rwkv_w0[l], rwkv_a0[l], rwkv_k_k[l], rwkv_k_a[l], rwkv_r_k[l].reshape(-1), v0,
                          jnp.zeros((MIX_W,), F32), jnp.zeros((MIX_W,), F32)])
        w2p = _pad_rows(rwkv_w2[l], 128, 0).astype(BF16)
        a2p = _pad_rows(rwkv_a2[l], 128, 64).astype(BF16)
        g2p = _pad_rows(rwkv_g2[l], 256, 0).astype(BF16)
        vres = None
        if l > 0:
            v1p = jnp.pad(rwkv_v1[l - 1], ((0, 0), (0, 128 - 32))).astype(BF16)
            v2p = _pad_rows(rwkv_v2[l - 1], 128, 0).astype(BF16)
            vres = (v_first, v1p, v2p)
        r, k, v, lw, kk, b, g, bonus = _rwkv_prep(seq_len, feats, mu, vecs, w2p, a2p, g2p, ones, vres)
        if l == 0:
            v_first = v
        y_scan = _rwkv_scan(seq_len, r, lw, k, v, kk, b)

        tile6 = lambda a: jnp.tile(a, N_HEADS).reshape(1, MIX_W)
        fb = jnp.pad(fox_f_b[l], (0, FF_PAD - N_HEADS)).reshape(1, FF_PAD)
        qg, kg = tile6(fox_q_g[l]), tile6(fox_k_g[l])
        q_n, k_n, v_b, ct, cend = _fox_prep(seq_len, fox, ff, qg, kg, fb, ones, FOX_BLOCK)
        attn = _fox_attn(seq_len, q_n, k_n, v_b, ct, cend, qg, kg, FOX_BLOCK)

        ln = jnp.stack([rwkv_ln_w[l], rwkv_ln_b[l]] + [jnp.zeros((MIX_W,), F32)] * 6)
        pw = jax.scipy.linalg.block_diag(*[pool_w[l, gi] for gi in range(4)]).astype(BF16)
        xf = _mix_out(seq_len, xf, y_scan, g, bonus, attn, fox, pool_in, ln, ones, pw,
                      row(pool_scale[l]), w_out[l].astype(BF16), row(mix_post_g[l]))
        xf = _ffn(xf, row(ffn_pre_g[l]), ffn_w_gu[l].astype(BF16), ffn_w_down[l].astype(BF16),
                  row(ffn_post_g[l]))
    return xf.reshape(batch, seq_len, D_MODEL)
```

```python
import functools

import jax
import jax.numpy as jnp
from jax import lax
from jax.experimental import pallas as pl
from jax.experimental.pallas import tpu as pltpu

F32 = jnp.float32
BF16 = jnp.bfloat16

D_MODEL = 1024
HEAD_DIM = 64
PAIR = 2 * HEAD_DIM
N_HEADS = 6
N_PAIRS = N_HEADS // 2
MIX_W = N_HEADS * HEAD_DIM
POOL_W = 256
POOL_HALO = 16
FFN_HIDDEN = 2816
FFN_CHUNK = 256
RMS_EPS = 1e-6
GN_EPS = HEAD_DIM * 1e-5

RWKV_IN = 1440
RWKV_PAD = 1536
FOX_IN = 4 * MIX_W
FF_PAD = 128
IN_SPLITS = ((0, 1536), (1536, 3072), (3072, 3200), (3200, 3456))
IN_TOTAL = 3456

LOG2E = 1.4426950408889634
FOX_BLOCK = 256
FOX_WIDE = 4
UNDERFLOW_CUT = 110.0
BF16_MARGIN = 1.02
SCAN_CHUNK = 64
VMEM_LIMIT = 56 * 1024 * 1024


def _params(*sem):
    return pltpu.CompilerParams(dimension_semantics=sem, vmem_limit_bytes=VMEM_LIMIT)


def _dot(a, b):
    return jnp.dot(a.astype(BF16), b.astype(BF16), preferred_element_type=F32)


def _dot_nt(a, b):
    return lax.dot_general(a.astype(BF16), b.astype(BF16), (((1,), (1,)), ((), ())),
                           preferred_element_type=F32)


def _split3(x):
    hi = x.astype(BF16)
    r1 = x - hi.astype(F32)
    mid = r1.astype(BF16)
    lo = (r1 - mid.astype(F32)).astype(BF16)
    return hi, mid, lo


def _head_sums(x):
    left = lax.broadcasted_iota(jnp.int32, (1, PAIR), 1) < HEAD_DIM
    outs = []
    for p in range(x.shape[1] // PAIR):
        xp = x[:, p * PAIR:(p + 1) * PAIR]
        lsum = jnp.sum(jnp.where(left, xp, 0.0), axis=1, keepdims=True)
        rsum = jnp.sum(jnp.where(left, 0.0, xp), axis=1, keepdims=True)
        outs.append(jnp.where(left, lsum, rsum))
    return jnp.concatenate(outs, axis=1)


def _dot3r(w, x):
    hi, mid, lo = _split3(x)
    d = lambda t: jnp.dot(w, t, preferred_element_type=F32)
    return d(hi) + d(mid) + d(lo)


def _rms(x, g):
    ms = jnp.mean(x * x, axis=-1, keepdims=True)
    return x * lax.rsqrt(ms + RMS_EPS) * g


def _softplus(z):
    return jnp.maximum(z, 0.0) + jnp.log1p(jnp.exp(-jnp.abs(z)))


def _sigmoid(z):
    return 1.0 / (1.0 + jnp.exp(-z))


def _tri(n, inclusive, dtype):
    r = lax.broadcasted_iota(jnp.int32, (n, n), 0)
    c = lax.broadcasted_iota(jnp.int32, (n, n), 1)
    keep = (r >= c) if inclusive else (r > c)
    return jnp.where(keep, 1.0, 0.0).astype(dtype)


def _w_regroup_body(w_ref, o_ref):
    w = w_ref[0]
    zeros = lambda c: jnp.zeros((w.shape[0], c), F32)
    fox0, ff0, pool0 = RWKV_IN, RWKV_IN + FOX_IN, RWKV_IN + FOX_IN + N_HEADS
    o_ref[0] = jnp.concatenate(
        [w[:, :fox0], zeros(RWKV_PAD - RWKV_IN), w[:, fox0:ff0], w[:, ff0:pool0],
         zeros(FF_PAD - N_HEADS), w[:, pool0:]], axis=1).astype(BF16)


def _w_regroup(w_in, rows=128):
    depth, d, cols = w_in.shape
    return pl.pallas_call(
        _w_regroup_body,
        grid=(depth, d // rows),
        in_specs=[pl.BlockSpec((1, rows, cols), lambda l, i: (l, i, 0))],
        out_specs=pl.BlockSpec((1, rows, IN_TOTAL), lambda l, i: (l, i, 0)),
        out_shape=jax.ShapeDtypeStruct((depth, d, IN_TOTAL), BF16),
        compiler_params=_params("parallel", "parallel"),
        name="w_regroup",
    )(w_in)


def _inproj_body(x_ref, g_ref, w_ref, o_rwkv, o_fox, o_ff, o_pool):
    h = _rms(x_ref[...], g_ref[...]).astype(BF16)
    for o_ref, (c0, c1) in zip((o_rwkv, o_fox, o_ff, o_pool), IN_SPLITS):
        o_ref[...] = jnp.dot(h, w_ref[:, c0:c1], preferred_element_type=F32)


def _inproj(x, g, w, tm=256):
    n = x.shape[0]
    widths = [c1 - c0 for c0, c1 in IN_SPLITS]
    return pl.pallas_call(
        _inproj_body,
        grid=(n // tm,),
        in_specs=[pl.BlockSpec((tm, D_MODEL), lambda i: (i, 0)),
                  pl.BlockSpec((1, D_MODEL), lambda i: (0, 0)),
                  pl.BlockSpec((D_MODEL, IN_TOTAL), lambda i: (0, 0))],
        out_specs=[pl.BlockSpec((tm, w_), lambda i: (i, 0)) for w_ in widths],
        out_shape=[jax.ShapeDtypeStruct((n, w_), F32) for w_ in widths],
        compiler_params=_params("parallel"),
        name="inproj",
    )(x, g, w)


def _rwkv_prep_body(seq_len, has_vres, *refs):
    if has_vres:
        (f_ref, halo_ref, mu_ref, vec_ref, w2_ref, a2_ref, g2_ref, vf_ref, v1_ref, v2_ref,
         r_o, k_o, v_o, lw_o, kk_o, b_o, g_o, bonus_o) = refs
    else:
        (f_ref, halo_ref, mu_ref, vec_ref, w2_ref, a2_ref, g2_ref,
         r_o, k_o, v_o, lw_o, kk_o, b_o, g_o, bonus_o) = refs
    tm = f_ref.shape[0]
    at_start = lax.rem(pl.program_id(0) * tm, seq_len) == 0
    f = f_ref[...]
    halo = jnp.where(at_start, 0.0, halo_ref[7:8, :])
    row = lax.broadcasted_iota(jnp.int32, (tm, 1), 0)
    prev = jnp.where(row == 0, halo, pltpu.roll(f, 1, axis=0))
    f = f + (prev - f) * mu_ref[...]
    r, k, v = f[:, 0:384], f[:, 384:768], f[:, 768:1152]
    wa = f[:, 1152:1280]
    gd = f[:, 1280:1536]
    w0, a0, k_k, k_a, r_k, v0 = (vec_ref[j:j + 1, :] for j in range(6))

    w_log = -_softplus(-(w0 + _dot(jnp.tanh(wa), w2_ref[...]))) - 0.5
    lw_o[...] = -jnp.exp(w_log)
    a = _sigmoid(a0 + _dot(wa, a2_ref[...]))
    g_o[...] = _dot(_sigmoid(gd), g2_ref[...])
    if has_vres:
        gate = _sigmoid(v0 + _dot(_dot(v, v1_ref[...]), v2_ref[...]))
        v = v + (vf_ref[...] - v) * gate
    kk = k * k_k
    ss = _head_sums(kk * kk)
    kk = kk / jnp.maximum(jnp.sqrt(ss), 1e-12)
    k = k * (1.0 + (a - 1.0) * k_a)
    r_o[...] = r
    k_o[...] = k
    v_o[...] = v
    kk_o[...] = kk
    b_o[...] = kk * a
    bonus_o[...] = _head_sums(r * k * r_k) * v


def _rwkv_prep(seq_len, feats, mu, vecs, w2p, a2p, g2p, vres, tm=256):
    n = feats.shape[0]
    full = lambda a: pl.BlockSpec(a.shape, lambda i: (0,) * a.ndim)
    tile = pl.BlockSpec((tm, MIX_W), lambda i: (i, 0))
    in_specs = [pl.BlockSpec((tm, RWKV_PAD), lambda i: (i, 0)),
                pl.BlockSpec((8, RWKV_PAD), lambda i: (jnp.maximum(i * (tm // 8) - 1, 0), 0)),
                full(mu), full(vecs), full(w2p), full(a2p), full(g2p)]
    args = [feats, feats, mu, vecs, w2p, a2p, g2p]
    if vres is not None:
        v_first, v1p, v2p = vres
        in_specs += [tile, full(v1p), full(v2p)]
        args += [v_first, v1p, v2p]
    return pl.pallas_call(
        functools.partial(_rwkv_prep_body, seq_len, vres is not None),
        grid=(n // tm,),
        in_specs=in_specs,
        out_specs=[tile] * 8,
        out_shape=[jax.ShapeDtypeStruct((n, MIX_W), F32)] * 8,
        compiler_params=_params("parallel"),
        name="rwkv_prep",
    )(*args)


def _scan_body(chunks_per_step, r_ref, lw_ref, k_ref, v_ref, kk_ref, b_ref, y_ref, h_ref):
    c_len = SCAN_CHUNK
    n_batch = r_ref.shape[0]

    @pl.when(pl.program_id(0) == 0)
    def _():
        h_ref[...] = jnp.zeros_like(h_ref)

    lane = lax.broadcasted_iota(jnp.int32, (1, PAIR), 1)
    left = lane < HEAD_DIM
    ri = lax.broadcasted_iota(jnp.int32, (PAIR, PAIR), 0)
    ci = lax.broadcasted_iota(jnp.int32, (PAIR, PAIR), 1)
    strict, incl, diag = ri > ci, ri >= ci, ri == ci
    tri = _tri(c_len, True, BF16)
    eye = jnp.where(diag, 1.0, 0.0)

    def stack(x):
        return jnp.concatenate([jnp.where(left, x, 0.0), jnp.where(left, 0.0, x)], axis=0)

    seqs = [(bi, p) for bi in range(n_batch) for p in range(N_PAIRS)]
    items = [(bi, p, c) for (bi, p) in seqs for c in range(chunks_per_step)]
    window = lambda bi, p, c: (bi, slice(c * c_len, (c + 1) * c_len), slice(p * PAIR, (p + 1) * PAIR))
    load = lambda ref: [ref[window(*it)] for it in items]
    each = lambda f, *cols: [f(*xs) for xs in zip(*cols)]

    lw = load(lw_ref)
    cum = each(lambda x: _dot3r(tri, x), lw)
    cum_end = each(lambda x: x[c_len - 1:c_len, :], cum)
    e_neg = each(lambda x: jnp.exp(-x), cum)
    e_tail = each(lambda x, xe: jnp.exp(xe - x), cum, cum_end)
    kk, b, k = load(kk_ref), load(b_ref), load(k_ref)
    l_a = each(lambda x, c_, w_: stack(-x * jnp.exp(c_ - w_)), kk, cum, lw)
    l_r = each(lambda x, c_: stack(x * jnp.exp(c_)), load(r_ref), cum)
    r_b = each(lambda x, e: stack(x * e), b, e_neg)
    r_k = each(lambda x, e: stack(x * e), k, e_neg)
    rb_t = each(lambda x, e: stack(x * e).T, b, e_tail)
    rk_t = each(lambda x, e: stack(x * e).T, k, e_tail)
    v_s = each(stack, load(v_ref))

    a_ab = each(lambda x, y: jnp.where(strict, _dot_nt(x, y), 0.0), l_a, r_b)
    a_ak = each(lambda x, y: jnp.where(strict, _dot_nt(x, y), 0.0), l_a, r_k)
    a_rb = each(lambda x, y: jnp.where(incl, _dot_nt(x, y), 0.0), l_r, r_b)
    a_rk = each(lambda x, y: jnp.where(incl, _dot_nt(x, y), 0.0), l_r, r_k)

    x = a_ab
    t_inv = each(lambda a: eye + a, x)
    for _ in range(5):
        x = each(lambda a: _dot(a, a), x)
        t_inv = each(lambda t, a: t + _dot(t, a), t_inv, x)

    t_a = each(_dot, t_inv, l_a)
    w_ = each(lambda t, a, v: _dot(t, _dot(a, v)), t_inv, a_ak, v_s)
    r_q = each(lambda l, a, t: l + _dot(a, t), l_r, a_rb, t_a)
    y_0 = each(lambda a, w, a2, v: _dot(a, w) + _dot(a2, v), a_rb, w_, a_rk, v_s)
    m_ = each(lambda ce, bt, t: jnp.where(diag, jnp.exp(ce), 0.0) + _dot(bt, t), cum_end, rb_t, t_a)
    n_ = each(lambda bt, w, kt, v: _dot(bt, w) + _dot(kt, v), rb_t, w_, rk_t, v_s)

    for s, (bi, p) in enumerate(seqs):
        h = h_ref[s]
        for c in range(chunks_per_step):
            idx = s * chunks_per_step + c
            y_st = _dot(r_q[idx], h) + y_0[idx]
            y_ref[window(bi, p, c)] = y_st[:c_len] + y_st[c_len:]
            h = _dot(m_[idx], h) + n_[idx]
        h_ref[s] = h


def _rwkv_scan(seq_len, r, lw, k, v, kk, b, chunks_per_step=2):
    n = r.shape[0]
    n_batch = n // seq_len
    rows = chunks_per_step * SCAN_CHUNK
    spec = pl.BlockSpec((n_batch, rows, MIX_W), lambda j: (0, j, 0))
    args = [a.reshape(n_batch, seq_len, MIX_W) for a in (r, lw, k, v, kk, b)]
    y = pl.pallas_call(
        functools.partial(_scan_body, chunks_per_step),
        grid=(seq_len // rows,),
        in_specs=[spec] * 6,
        out_specs=spec,
        out_shape=jax.ShapeDtypeStruct((n_batch, seq_len, MIX_W), F32),
        scratch_shapes=[pltpu.VMEM((n_batch * N_PAIRS, PAIR, PAIR), F32)],
        compiler_params=_params("arbitrary"),
        name="rwkv_scan",
    )(*args)
    return y.reshape(n, MIX_W)


def _fox_prep_body(seq_len, q_ref, k_ref, v_ref, ff_ref, qg_ref, kg_ref, fb_ref,
                   q_o, k_o, v_o, ct_o, cend_o, carry_ref):
    tm = q_ref.shape[0]
    blocks = seq_len // tm
    j = lax.rem(pl.program_id(0), blocks)

    @pl.when(j == 0)
    def _():
        carry_ref[...] = jnp.zeros_like(carry_ref)
        cend_o[...] = jnp.zeros_like(cend_o)

    def qk_norm(x, g):
        ms = _head_sums(x * x) * (1.0 / HEAD_DIM)
        return x * lax.rsqrt(ms + RMS_EPS) * g

    q_o[...] = (qk_norm(q_ref[...], qg_ref[...]) * (LOG2E * HEAD_DIM ** -0.5)).astype(BF16)
    k_o[...] = qk_norm(k_ref[...], kg_ref[...]).astype(BF16)
    v_o[...] = v_ref[...].astype(BF16)
    log_f = -_softplus(-(ff_ref[...] + fb_ref[...]))
    cum = _dot3r(_tri(tm, True, BF16), log_f) + carry_ref[...]
    carry_ref[...] = cum[tm - 1:tm, :]
    cum_t = cum.T[0:8, :]
    ct_o[0] = cum_t
    lane = lax.broadcasted_iota(jnp.int32, (1, blocks), 1)
    cend_o[0] = jnp.where(lane == j, cum_t[:, tm - 1:tm], cend_o[0])


def _fox_prep(seq_len, fox, ff, qg, kg, fb, tm):
    n = fox.shape[0]
    blocks = seq_len // tm
    full = lambda a: pl.BlockSpec(a.shape, lambda i: (0,) * a.ndim)
    col = lambda j: pl.BlockSpec((tm, MIX_W), lambda i: (i, j))
    tile = pl.BlockSpec((tm, MIX_W), lambda i: (i, 0))
    return pl.pallas_call(
        functools.partial(_fox_prep_body, seq_len),
        grid=(n // tm,),
        in_specs=[col(0), col(1), col(2), pl.BlockSpec((tm, FF_PAD), lambda i: (i, 0)),
                  full(qg), full(kg), full(fb)],
        out_specs=[tile, tile, tile, pl.BlockSpec((1, 8, tm), lambda i: (i, 0, 0)),
                   pl.BlockSpec((1, 8, blocks), lambda i: (i // blocks, 0, 0))],
        out_shape=[jax.ShapeDtypeStruct((n, MIX_W), BF16)] * 3
        + [jax.ShapeDtypeStruct((n // tm, 8, tm), F32),
           jax.ShapeDtypeStruct((n // seq_len, 8, blocks), F32)],
        scratch_shapes=[pltpu.VMEM((1, FF_PAD), F32)],
        compiler_params=_params("arbitrary"),
        name="fox_prep",
    )(fox, fox, fox, ff, qg, kg, fb)


def _fox_attn_body(q_ref, k_ref, v_ref, ct_ref, cend_ref, qg_ref, kg_ref, o_ref):
    p = pl.program_id(1)
    i = pl.program_id(2)
    t = q_ref.shape[0]
    nq = cend_ref.shape[2]
    lane = lax.broadcasted_iota(jnp.int32, (1, PAIR), 1)
    left = lane < HEAD_DIM
    q = q_ref[...]
    q_h = (jnp.where(left, q, jnp.zeros_like(q)), jnp.where(left, jnp.zeros_like(q), q))
    sub = lax.broadcasted_iota(jnp.int32, (8, 1), 0)
    row_of_head = lambda x, h: jnp.sum(jnp.where(sub == 2 * p + h, x, 0.0), axis=0, keepdims=True)

    amax = lambda r: jnp.max(jnp.abs(r[...]), axis=1, keepdims=True)
    qk_bound = 8.0 * BF16_MARGIN * amax(qg_ref) * amax(kg_ref)
    blk = lax.broadcasted_iota(jnp.int32, (1, nq), 1)
    cend = cend_ref[0]
    c_ref = []
    needed = blk < 0
    for h in (0, 1):
        ce = row_of_head(cend, h)
        c_ref.append(jnp.sum(jnp.where(blk == i - 1, ce, 0.0), axis=1, keepdims=True))
        needed = needed | ((blk < i) & (c_ref[h] - ce > -(2.0 * qk_bound + UNDERFLOW_CUT)))
    first = i - jnp.sum(jnp.where(needed, 1, 0))

    def block(j, width, carry, diagonal):
        start = pl.multiple_of(j * t, t)
        k_j = k_ref[pl.ds(start, width * t), :]
        v_j = v_ref[pl.ds(start, width * t), :]
        cts = [ct_ref[j + u] for u in range(width)]
        acc = carry[4]
        new = []
        upd = []
        for h in (0, 1):
            m, l = carry[2 * h], carry[2 * h + 1]
            c_k = jnp.concatenate([row_of_head(ct, h) for ct in cts], axis=1)
            z = _dot_nt(q_h[h], k_j) - (c_k - c_ref[h]) * LOG2E
            if diagonal:
                rr = lax.broadcasted_iota(jnp.int32, (t, t), 0)
                cc = lax.broadcasted_iota(jnp.int32, (t, t), 1)
                z = jnp.where(cc <= rr, z, -1e30)
            m_new = jnp.maximum(m, jnp.max(z, axis=1, keepdims=True))
            pr = jnp.exp2(z - m_new)
            alpha = jnp.exp2(m - m_new)
            new += [m_new, l * alpha + jnp.sum(pr, axis=1, keepdims=True)]
            upd.append(acc * alpha + jnp.dot(pr.astype(BF16), v_j, preferred_element_type=F32))
        return (*new, jnp.where(left, upd[0], upd[1]))

    neg = jnp.full((t, 1), -1e30, F32)
    zero = jnp.zeros((t, 1), F32)
    carry = block(i, 1, (neg, zero, neg, zero, jnp.zeros((t, PAIR), F32)), True)
    wide = jnp.minimum((i - first + FOX_WIDE - 1) // FOX_WIDE, i // FOX_WIDE)
    lo = i - wide * FOX_WIDE
    carry = lax.fori_loop(0, wide, lambda g, c: block(lo + g * FOX_WIDE, FOX_WIDE, c, False), carry)
    carry = lax.fori_loop(first, lo, lambda j, c: block(j, 1, c, False), carry)
    o_ref[...] = jnp.where(left, carry[4] / carry[1], carry[4] / carry[3])


def _fox_attn(seq_len, q, k, v, ct, cend, qg, kg, t):
    n = q.shape[0]
    nq = seq_len // t
    full = lambda a: pl.BlockSpec(a.shape, lambda bi, p, i: (0,) * a.ndim)
    return pl.pallas_call(
        _fox_attn_body,
        grid=(n // seq_len, N_PAIRS, nq),
        in_specs=[pl.BlockSpec((t, PAIR), lambda bi, p, i: (bi * nq + i, p)),
                  pl.BlockSpec((seq_len, PAIR), lambda bi, p, i: (bi, p)),
                  pl.BlockSpec((seq_len, PAIR), lambda bi, p, i: (bi, p)),
                  pl.BlockSpec((nq, 8, t), lambda bi, p, i: (bi, 0, 0)),
                  pl.BlockSpec((1, 8, nq), lambda bi, p, i: (bi, 0, 0)),
                  full(qg), full(kg)],
        out_specs=pl.BlockSpec((t, PAIR), lambda bi, p, i: (bi * nq + i, p)),
        out_shape=jax.ShapeDtypeStruct((n, MIX_W), F32),
        compiler_params=_params("parallel", "parallel", "arbitrary"),
        name="fox_attn",
    )(q, k, v, ct, cend, qg, kg)


def _mix_out_body(seq_len, x_ref, ys_ref, g_ref, bonus_ref, at_ref, fg_ref, pool_ref, halo_ref,
                  ln_ref, pw_ref, ps_ref, wo_ref, pg_ref, o_ref):
    tm = x_ref.shape[0]
    row0 = lax.rem(pl.program_id(0) * tm, seq_len)

    y = ys_ref[...]
    d = y - _head_sums(y) * (1.0 / HEAD_DIM)
    var = _head_sums(d * d) * (1.0 / HEAD_DIM)
    y_rwkv = (d * lax.rsqrt(var + GN_EPS) * ln_ref[0:1, :] + ln_ref[1:2, :] + bonus_ref[...]) * g_ref[...]

    y_fox = at_ref[...] * _sigmoid(fg_ref[...])

    pin = pool_ref[...]
    halo = jnp.where(row0 == 0, 0.0, halo_ref[...])
    s = jnp.concatenate([halo, pin], axis=0)
    sums = []
    for shift in (1, 2, 4, 8):
        s = s + pltpu.roll(s, shift, axis=0)
        sums.append(s[POOL_HALO:, :])
    grp = lax.broadcasted_iota(jnp.int32, (1, POOL_W), 1) // HEAD_DIM
    win_sum = jnp.where(grp == 0, sums[0], jnp.where(grp == 1, sums[1], jnp.where(grp == 2, sums[2], sums[3])))
    win = jnp.where(grp == 0, 2.0, jnp.where(grp == 1, 4.0, jnp.where(grp == 2, 8.0, 16.0)))
    pos = (row0 + lax.broadcasted_iota(jnp.int32, (tm, 1), 0) + 1).astype(F32)
    u = win_sum / jnp.minimum(pos, win) - pin
    y_pool = _dot(u, pw_ref[...]) * ps_ref[...]

    mixed = (_dot(y_rwkv, wo_ref[0:384, :]) + _dot(y_fox, wo_ref[384:768, :])
             + _dot(y_pool, wo_ref[768:1024, :]))
    o_ref[...] = x_ref[...] + _rms(mixed, pg_ref[...])


def _mix_out(seq_len, x, y_scan, g, bonus, attn, fox, pool_in, ln, pw, ps, wo, pg, tm=256):
    n = x.shape[0]
    full = lambda a: pl.BlockSpec(a.shape, lambda i: (0,) * a.ndim)
    tile = pl.BlockSpec((tm, MIX_W), lambda i: (i, 0))
    return pl.pallas_call(
        functools.partial(_mix_out_body, seq_len),
        grid=(n // tm,),
        in_specs=[pl.BlockSpec((tm, D_MODEL), lambda i: (i, 0)), tile, tile, tile, tile,
                  pl.BlockSpec((tm, MIX_W), lambda i: (i, 3)),
                  pl.BlockSpec((tm, POOL_W), lambda i: (i, 0)),
                  pl.BlockSpec((POOL_HALO, POOL_W),
                               lambda i: (jnp.maximum(i * (tm // POOL_HALO) - 1, 0), 0)),
                  full(ln), full(pw), full(ps), full(wo), full(pg)],
        out_specs=pl.BlockSpec((tm, D_MODEL), lambda i: (i, 0)),
        out_shape=jax.ShapeDtypeStruct((n, D_MODEL), F32),
        compiler_params=_params("parallel"),
        name="mix_out",
    )(x, y_scan, g, bonus, attn, fox, pool_in, pool_in, ln, pw, ps, wo, pg)


def _ffn_body(x_ref, g1_ref, wgu_ref, wd_ref, g2_ref, o_ref):
    x = x_ref[...]
    h = _rms(x, g1_ref[...]).astype(BF16)
    acc = jnp.zeros(x.shape, F32)
    for c in range(FFN_HIDDEN // FFN_CHUNK):
        c0 = c * FFN_CHUNK
        gate = jnp.dot(h, wgu_ref[:, c0:c0 + FFN_CHUNK], preferred_element_type=F32)
        up = jnp.dot(h, wgu_ref[:, FFN_HIDDEN + c0:FFN_HIDDEN + c0 + FFN_CHUNK],
                     preferred_element_type=F32)
        act = (gate * _sigmoid(gate) * up).astype(BF16)
        acc = acc + jnp.dot(act, wd_ref[c0:c0 + FFN_CHUNK, :], preferred_element_type=F32)
    o_ref[...] = x + _rms(acc, g2_ref[...])


def _ffn(x, g1, wgu, wd, g2, tm=512):
    n = x.shape[0]
    full = lambda a: pl.BlockSpec(a.shape, lambda i: (0,) * a.ndim)
    tile = pl.BlockSpec((tm, D_MODEL), lambda i: (i, 0))
    return pl.pallas_call(
        _ffn_body,
        grid=(n // tm,),
        in_specs=[tile, full(g1), full(wgu), full(wd), full(g2)],
        out_specs=tile,
        out_shape=jax.ShapeDtypeStruct((n, D_MODEL), F32),
        compiler_params=_params("parallel"),
        name="ffn",
    )(x, g1, wgu, wd, g2)


def _pad_rows(a, rows, at=0):
    out = jnp.zeros((rows, a.shape[1]), a.dtype)
    return lax.dynamic_update_slice(out, a, (at, 0))


def kernel(x, mix_pre_g, mix_post_g, ffn_pre_g, ffn_post_g, w_in, w_out, rwkv_mu, rwkv_w0, rwkv_w2,
           rwkv_a0, rwkv_a2, rwkv_g2, rwkv_v0, rwkv_v1, rwkv_v2, rwkv_k_k, rwkv_k_a, rwkv_r_k,
           rwkv_ln_w, rwkv_ln_b, fox_q_g, fox_k_g, fox_f_b, pool_w, pool_scale, ffn_w_gu, ffn_w_down):
    batch, seq_len, _ = x.shape
    n = batch * seq_len
    depth = w_in.shape[0]
    row = lambda a: a.reshape(1, -1).astype(F32)
    xf = x.reshape(n, D_MODEL)
    w_arr = _w_regroup(w_in)
    v_first = None
    for l in range(depth):
        feats, fox, ff, pool_in = _inproj(xf, row(mix_pre_g[l]), w_arr[l])

        mu = jnp.pad(rwkv_mu[l], (0, RWKV_PAD - RWKV_IN)).reshape(1, RWKV_PAD)
        v0 = rwkv_v0[l - 1] if l > 0 else jnp.zeros((MIX_W,), F32)
        vecs = jnp.stack([rwkv_w0[l], rwkv_a0[l], rwkv_k_k[l], rwkv_k_a[l], rwkv_r_k[l].reshape(-1), v0,
                          jnp.zeros((MIX_W,), F32), jnp.zeros((MIX_W,), F32)])
        w2p = _pad_rows(rwkv_w2[l], 128, 0).astype(BF16)
        a2p = _pad_rows(rwkv_a2[l], 128, 64).astype(BF16)
        g2p = _pad_rows(rwkv_g2[l], 256, 0).astype(BF16)
        vres = None
        if l > 0:
            v1p = jnp.pad(rwkv_v1[l - 1], ((0, 0), (0, 128 - 32))).astype(BF16)
            v2p = _pad_rows(rwkv_v2[l - 1], 128, 0).astype(BF16)
            vres = (v_first, v1p, v2p)
        r, k, v, lw, kk, b, g, bonus = _rwkv_prep(seq_len, feats, mu, vecs, w2p, a2p, g2p, vres)
        if l == 0:
            v_first = v
        y_scan = _rwkv_scan(seq_len, r, lw, k, v, kk, b)

        tile6 = lambda a: jnp.tile(a, N_HEADS).reshape(1, MIX_W)
        fb = jnp.pad(fox_f_b[l], (0, FF_PAD - N_HEADS)).reshape(1, FF_PAD)
        qg, kg = tile6(fox_q_g[l]), tile6(fox_k_g[l])
        q_n, k_n, v_b, ct, cend = _fox_prep(seq_len, fox, ff, qg, kg, fb, FOX_BLOCK)
        attn = _fox_attn(seq_len, q_n, k_n, v_b, ct, cend, qg, kg, FOX_BLOCK)

        ln = jnp.stack([rwkv_ln_w[l], rwkv_ln_b[l]] + [jnp.zeros((MIX_W,), F32)] * 6)
        pw = jax.scipy.linalg.block_diag(*[pool_w[l, gi] for gi in range(4)]).astype(BF16)
        xf = _mix_out(seq_len, xf, y_scan, g, bonus, attn, fox, pool_in, ln, pw,
                      row(pool_scale[l]), w_out[l].astype(BF16), row(mix_post_g[l]))
        xf = _ffn(xf, row(ffn_pre_g[l]), ffn_w_gu[l].astype(BF16), ffn_w_down[l].astype(BF16),
                  row(ffn_post_g[l]))
    return xf.reshape(batch, seq_len, D_MODEL)
```

```python
import functools

import jax
import jax.numpy as jnp
from jax import lax
from jax.experimental import pallas as pl
from jax.experimental.pallas import tpu as pltpu

F32 = jnp.float32
BF16 = jnp.bfloat16

D_MODEL = 1024
HEAD_DIM = 64
PAIR = 2 * HEAD_DIM
N_HEADS = 6
N_PAIRS = N_HEADS // 2
MIX_W = N_HEADS * HEAD_DIM
POOL_W = 256
POOL_HALO = 16
FFN_HIDDEN = 2816
FFN_CHUNK = 256
RMS_EPS = 1e-6
GN_EPS = HEAD_DIM * 1e-5

RWKV_IN = 1440
RWKV_PAD = 1536
FOX_IN = 4 * MIX_W
FF_PAD = 128
IN_SPLITS = ((0, 1536), (1536, 3072), (3072, 3200), (3200, 3456))
IN_TOTAL = 3456

LOG2E = 1.4426950408889634
FOX_BLOCK = 256
FOX_Q_BLOCKS = 2
FOX_WIDE = 4
UNDERFLOW_CUT = 110.0
BF16_MARGIN = 1.02
SCAN_CHUNK = 64
VMEM_LIMIT = 56 * 1024 * 1024


def _params(*sem):
    return pltpu.CompilerParams(dimension_semantics=sem, vmem_limit_bytes=VMEM_LIMIT)


def _dot(a, b):
    return jnp.dot(a.astype(BF16), b.astype(BF16), preferred_element_type=F32)


def _dot_nt(a, b):
    return lax.dot_general(a.astype(BF16), b.astype(BF16), (((1,), (1,)), ((), ())),
                           preferred_element_type=F32)


def _split3(x):
    hi = x.astype(BF16)
    r1 = x - hi.astype(F32)
    mid = r1.astype(BF16)
    lo = (r1 - mid.astype(F32)).astype(BF16)
    return hi, mid, lo


def _head_sums(x):
    left = lax.broadcasted_iota(jnp.int32, (1, PAIR), 1) < HEAD_DIM
    outs = []
    for p in range(x.shape[1] // PAIR):
        xp = x[:, p * PAIR:(p + 1) * PAIR]
        lsum = jnp.sum(jnp.where(left, xp, 0.0), axis=1, keepdims=True)
        rsum = jnp.sum(jnp.where(left, 0.0, xp), axis=1, keepdims=True)
        outs.append(jnp.where(left, lsum, rsum))
    return jnp.concatenate(outs, axis=1)


def _dot3r(w, x):
    hi, mid, lo = _split3(x)
    d = lambda t: jnp.dot(w, t, preferred_element_type=F32)
    return d(hi) + d(mid) + d(lo)


def _rms(x, g):
    ms = jnp.mean(x * x, axis=-1, keepdims=True)
    return x * lax.rsqrt(ms + RMS_EPS) * g


def _softplus(z):
    return jnp.maximum(z, 0.0) + jnp.log1p(jnp.exp(-jnp.abs(z)))


def _sigmoid(z):
    return 1.0 / (1.0 + jnp.exp(-z))


def _tri(n, inclusive, dtype):
    r = lax.broadcasted_iota(jnp.int32, (n, n), 0)
    c = lax.broadcasted_iota(jnp.int32, (n, n), 1)
    keep = (r >= c) if inclusive else (r > c)
    return jnp.where(keep, 1.0, 0.0).astype(dtype)


def _w_regroup_body(w_ref, o_ref):
    w = w_ref[0]
    zeros = lambda c: jnp.zeros((w.shape[0], c), F32)
    fox0, ff0, pool0 = RWKV_IN, RWKV_IN + FOX_IN, RWKV_IN + FOX_IN + N_HEADS
    o_ref[0] = jnp.concatenate(
        [w[:, :fox0], zeros(RWKV_PAD - RWKV_IN), w[:, fox0:ff0], w[:, ff0:pool0],
         zeros(FF_PAD - N_HEADS), w[:, pool0:]], axis=1).astype(BF16)


def _w_regroup(w_in, rows=128):
    depth, d, cols = w_in.shape
    return pl.pallas_call(
        _w_regroup_body,
        grid=(depth, d // rows),
        in_specs=[pl.BlockSpec((1, rows, cols), lambda l, i: (l, i, 0))],
        out_specs=pl.BlockSpec((1, rows, IN_TOTAL), lambda l, i: (l, i, 0)),
        out_shape=jax.ShapeDtypeStruct((depth, d, IN_TOTAL), BF16),
        compiler_params=_params("parallel", "parallel"),
        name="w_regroup",
    )(w_in)


def _inproj_body(x_ref, g_ref, w_ref, o_rwkv, o_fox, o_ff, o_pool):
    h = _rms(x_ref[...], g_ref[...]).astype(BF16)
    for o_ref, (c0, c1) in zip((o_rwkv, o_fox, o_ff, o_pool), IN_SPLITS):
        o_ref[...] = jnp.dot(h, w_ref[:, c0:c1], preferred_element_type=F32)


def _inproj(x, g, w, tm=256):
    n = x.shape[0]
    widths = [c1 - c0 for c0, c1 in IN_SPLITS]
    return pl.pallas_call(
        _inproj_body,
        grid=(n // tm,),
        in_specs=[pl.BlockSpec((tm, D_MODEL), lambda i: (i, 0)),
                  pl.BlockSpec((1, D_MODEL), lambda i: (0, 0)),
                  pl.BlockSpec((D_MODEL, IN_TOTAL), lambda i: (0, 0))],
        out_specs=[pl.BlockSpec((tm, w_), lambda i: (i, 0)) for w_ in widths],
        out_shape=[jax.ShapeDtypeStruct((n, w_), F32) for w_ in widths],
        compiler_params=_params("parallel"),
        name="inproj",
    )(x, g, w)


def _rwkv_prep_body(seq_len, has_vres, *refs):
    if has_vres:
        (f_ref, halo_ref, mu_ref, vec_ref, w2_ref, a2_ref, g2_ref, vf_ref, v1_ref, v2_ref,
         r_o, k_o, v_o, lw_o, kk_o, b_o, g_o, bonus_o) = refs
    else:
        (f_ref, halo_ref, mu_ref, vec_ref, w2_ref, a2_ref, g2_ref,
         r_o, k_o, v_o, lw_o, kk_o, b_o, g_o, bonus_o) = refs
    tm = f_ref.shape[0]
    at_start = lax.rem(pl.program_id(0) * tm, seq_len) == 0
    f = f_ref[...]
    halo = jnp.where(at_start, 0.0, halo_ref[7:8, :])
    row = lax.broadcasted_iota(jnp.int32, (tm, 1), 0)
    prev = jnp.where(row == 0, halo, pltpu.roll(f, 1, axis=0))
    f = f + (prev - f) * mu_ref[...]
    r, k, v = f[:, 0:384], f[:, 384:768], f[:, 768:1152]
    wa = f[:, 1152:1280]
    gd = f[:, 1280:1536]
    w0, a0, k_k, k_a, r_k, v0 = (vec_ref[j:j + 1, :] for j in range(6))

    w_log = -_softplus(-(w0 + _dot(jnp.tanh(wa), w2_ref[...]))) - 0.5
    lw_o[...] = -jnp.exp(w_log)
    a = _sigmoid(a0 + _dot(wa, a2_ref[...]))
    g_o[...] = _dot(_sigmoid(gd), g2_ref[...])
    if has_vres:
        gate = _sigmoid(v0 + _dot(_dot(v, v1_ref[...]), v2_ref[...]))
        v = v + (vf_ref[...] - v) * gate
    kk = k * k_k
    ss = _head_sums(kk * kk)
    kk = kk / jnp.maximum(jnp.sqrt(ss), 1e-12)
    k = k * (1.0 + (a - 1.0) * k_a)
    r_o[...] = r
    k_o[...] = k
    v_o[...] = v
    kk_o[...] = kk
    b_o[...] = kk * a
    bonus_o[...] = _head_sums(r * k * r_k) * v


def _rwkv_prep(seq_len, feats, mu, vecs, w2p, a2p, g2p, vres, tm=256):
    n = feats.shape[0]
    full = lambda a: pl.BlockSpec(a.shape, lambda i: (0,) * a.ndim)
    tile = pl.BlockSpec((tm, MIX_W), lambda i: (i, 0))
    in_specs = [pl.BlockSpec((tm, RWKV_PAD), lambda i: (i, 0)),
                pl.BlockSpec((8, RWKV_PAD), lambda i: (jnp.maximum(i * (tm // 8) - 1, 0), 0)),
                full(mu), full(vecs), full(w2p), full(a2p), full(g2p)]
    args = [feats, feats, mu, vecs, w2p, a2p, g2p]
    if vres is not None:
        v_first, v1p, v2p = vres
        in_specs += [tile, full(v1p), full(v2p)]
        args += [v_first, v1p, v2p]
    return pl.pallas_call(
        functools.partial(_rwkv_prep_body, seq_len, vres is not None),
        grid=(n // tm,),
        in_specs=in_specs,
        out_specs=[tile] * 8,
        out_shape=[jax.ShapeDtypeStruct((n, MIX_W), F32)] * 8,
        compiler_params=_params("parallel"),
        name="rwkv_prep",
    )(*args)


def _scan_body(chunks_per_step, r_ref, lw_ref, k_ref, v_ref, kk_ref, b_ref, y_ref, h_ref):
    c_len = SCAN_CHUNK
    n_batch = r_ref.shape[0]

    @pl.when(pl.program_id(0) == 0)
    def _():
        h_ref[...] = jnp.zeros_like(h_ref)

    lane = lax.broadcasted_iota(jnp.int32, (1, PAIR), 1)
    left = lane < HEAD_DIM
    ri = lax.broadcasted_iota(jnp.int32, (PAIR, PAIR), 0)
    ci = lax.broadcasted_iota(jnp.int32, (PAIR, PAIR), 1)
    strict, incl, diag = ri > ci, ri >= ci, ri == ci
    tri = _tri(c_len, True, BF16)
    eye = jnp.where(diag, 1.0, 0.0)

    def stack(x):
        return jnp.concatenate([jnp.where(left, x, 0.0), jnp.where(left, 0.0, x)], axis=0)

    seqs = [(bi, p) for bi in range(n_batch) for p in range(N_PAIRS)]
    items = [(bi, p, c) for (bi, p) in seqs for c in range(chunks_per_step)]
    window = lambda bi, p, c: (bi, slice(c * c_len, (c + 1) * c_len), slice(p * PAIR, (p + 1) * PAIR))
    load = lambda ref: [ref[window(*it)] for it in items]
    each = lambda f, *cols: [f(*xs) for xs in zip(*cols)]

    lw = load(lw_ref)
    cum = each(lambda x: _dot3r(tri, x), lw)
    cum_end = each(lambda x: x[c_len - 1:c_len, :], cum)
    e_neg = each(lambda x: jnp.exp(-x), cum)
    e_tail = each(lambda x, xe: jnp.exp(xe - x), cum, cum_end)
    kk, b, k = load(kk_ref), load(b_ref), load(k_ref)
    l_a = each(lambda x, c_, w_: stack(-x * jnp.exp(c_ - w_)), kk, cum, lw)
    l_r = each(lambda x, c_: stack(x * jnp.exp(c_)), load(r_ref), cum)
    r_b = each(lambda x, e: stack(x * e), b, e_neg)
    r_k = each(lambda x, e: stack(x * e), k, e_neg)
    rb_t = each(lambda x, e: stack(x * e).T, b, e_tail)
    rk_t = each(lambda x, e: stack(x * e).T, k, e_tail)
    v_s = each(stack, load(v_ref))

    a_ab = each(lambda x, y: jnp.where(strict, _dot_nt(x, y), 0.0), l_a, r_b)
    a_ak = each(lambda x, y: jnp.where(strict, _dot_nt(x, y), 0.0), l_a, r_k)
    a_rb = each(lambda x, y: jnp.where(incl, _dot_nt(x, y), 0.0), l_r, r_b)
    a_rk = each(lambda x, y: jnp.where(incl, _dot_nt(x, y), 0.0), l_r, r_k)

    x = a_ab
    t_inv = each(lambda a: eye + a, x)
    for _ in range(5):
        x = each(lambda a: _dot(a, a), x)
        t_inv = each(lambda t, a: t + _dot(t, a), t_inv, x)

    t_a = each(_dot, t_inv, l_a)
    w_ = each(lambda t, a, v: _dot(t, _dot(a, v)), t_inv, a_ak, v_s)
    r_q = each(lambda l, a, t: l + _dot(a, t), l_r, a_rb, t_a)
    y_0 = each(lambda a, w, a2, v: _dot(a, w) + _dot(a2, v), a_rb, w_, a_rk, v_s)
    m_ = each(lambda ce, bt, t: jnp.where(diag, jnp.exp(ce), 0.0) + _dot(bt, t), cum_end, rb_t, t_a)
    n_ = each(lambda bt, w, kt, v: _dot(bt, w) + _dot(kt, v), rb_t, w_, rk_t, v_s)

    for s, (bi, p) in enumerate(seqs):
        h = h_ref[s]
        for c in range(chunks_per_step):
            idx = s * chunks_per_step + c
            y_st = _dot(r_q[idx], h) + y_0[idx]
            y_ref[window(bi, p, c)] = y_st[:c_len] + y_st[c_len:]
            h = _dot(m_[idx], h) + n_[idx]
        h_ref[s] = h


def _rwkv_scan(seq_len, r, lw, k, v, kk, b, chunks_per_step=2):
    n = r.shape[0]
    n_batch = n // seq_len
    rows = chunks_per_step * SCAN_CHUNK
    spec = pl.BlockSpec((n_batch, rows, MIX_W), lambda j: (0, j, 0))
    args = [a.reshape(n_batch, seq_len, MIX_W) for a in (r, lw, k, v, kk, b)]
    y = pl.pallas_call(
        functools.partial(_scan_body, chunks_per_step),
        grid=(seq_len // rows,),
        in_specs=[spec] * 6,
        out_specs=spec,
        out_shape=jax.ShapeDtypeStruct((n_batch, seq_len, MIX_W), F32),
        scratch_shapes=[pltpu.VMEM((n_batch * N_PAIRS, PAIR, PAIR), F32)],
        compiler_params=_params("arbitrary"),
        name="rwkv_scan",
    )(*args)
    return y.reshape(n, MIX_W)


def _fox_prep_body(seq_len, q_ref, k_ref, v_ref, ff_ref, qg_ref, kg_ref, fb_ref,
                   q_o, k_o, vt_o, cend_o, carry_ref):
    tm = q_ref.shape[0]
    blocks = seq_len // tm
    j = lax.rem(pl.program_id(0), blocks)

    @pl.when(j == 0)
    def _():
        carry_ref[...] = jnp.zeros_like(carry_ref)
        cend_o[...] = jnp.zeros_like(cend_o)

    def qk_norm(x, g):
        ms = _head_sums(x * x) * (1.0 / HEAD_DIM)
        return x * lax.rsqrt(ms + RMS_EPS) * g

    lane = lax.broadcasted_iota(jnp.int32, (1, PAIR), 1)
    left = lane < HEAD_DIM

    def pad_heads(x, aug):
        cols = []
        for h in range(N_HEADS):
            src = x[:, (h // 2) * PAIR:(h // 2 + 1) * PAIR]
            if h % 2:
                src = pltpu.roll(src, HEAD_DIM, axis=1)
            cols.append(jnp.where(left, src, aug(h)))
        return jnp.concatenate(cols, axis=1).astype(BF16)

    log_f = -_softplus(-(ff_ref[...] + fb_ref[...]))
    local = _dot3r(_tri(tm, True, BF16), log_f)
    bias = local * (-LOG2E)

    def k_aug(h):
        col = jnp.sum(jnp.where(lane == h, bias, 0.0), axis=1, keepdims=True)
        hi, mid, lo = (p.astype(F32) for p in _split3(col))
        return jnp.where(lane == HEAD_DIM, hi,
                         jnp.where(lane == HEAD_DIM + 1, mid, jnp.where(lane == HEAD_DIM + 2, lo, 0.0)))

    q_aug = jnp.where((lane >= HEAD_DIM) & (lane < HEAD_DIM + 3), 1.0, 0.0)
    q_o[...] = pad_heads(qk_norm(q_ref[...], qg_ref[...]) * (LOG2E * HEAD_DIM ** -0.5), lambda h: q_aug)
    k_o[...] = pad_heads(qk_norm(k_ref[...], kg_ref[...]), k_aug)

    v_t = v_ref[...].T
    ones_row = jnp.where(lax.broadcasted_iota(jnp.int32, (HEAD_DIM, tm), 0) == 0, 1.0, 0.0)
    rows = []
    for h in range(N_HEADS):
        rows += [v_t[h * HEAD_DIM:(h + 1) * HEAD_DIM, :], ones_row]
    vt_o[0] = jnp.concatenate(rows, axis=0).astype(BF16)

    cum = local + carry_ref[...]
    carry_ref[...] = cum[tm - 1:tm, :]
    blk = lax.broadcasted_iota(jnp.int32, (1, blocks), 1)
    cend_o[0] = jnp.where(blk == j, cum.T[0:8, tm - 1:tm], cend_o[0])


def _fox_prep(seq_len, fox, ff, qg, kg, fb, tm):
    n = fox.shape[0]
    blocks = seq_len // tm
    full = lambda a: pl.BlockSpec(a.shape, lambda i: (0,) * a.ndim)
    col = lambda j: pl.BlockSpec((tm, MIX_W), lambda i: (i, j))
    padded = pl.BlockSpec((tm, N_HEADS * PAIR), lambda i: (i, 0))
    return pl.pallas_call(
        functools.partial(_fox_prep_body, seq_len),
        grid=(n // tm,),
        in_specs=[col(0), col(1), col(2), pl.BlockSpec((tm, FF_PAD), lambda i: (i, 0)),
                  full(qg), full(kg), full(fb)],
        out_specs=[padded, padded, pl.BlockSpec((1, N_HEADS * PAIR, tm), lambda i: (i, 0, 0)),
                   pl.BlockSpec((1, 8, blocks), lambda i: (i // blocks, 0, 0))],
        out_shape=[jax.ShapeDtypeStruct((n, N_HEADS * PAIR), BF16)] * 2
        + [jax.ShapeDtypeStruct((n // tm, N_HEADS * PAIR, tm), BF16),
           jax.ShapeDtypeStruct((n // seq_len, 8, blocks), F32)],
        scratch_shapes=[pltpu.VMEM((1, FF_PAD), F32)],
        compiler_params=_params("arbitrary"),
        name="fox_prep",
    )(fox, fox, fox, ff, qg, kg, fb)


def _fox_attn_body(q_ref, k_ref, vt_ref, cend_ref, qg_ref, kg_ref, o_ref):
    h = pl.program_id(1)
    tq = q_ref.shape[0]
    tk = vt_ref.shape[2]
    q_blocks = tq // tk
    diag = pl.program_id(2) * q_blocks
    nk = cend_ref.shape[2]
    q = q_ref[...]
    sub = lax.broadcasted_iota(jnp.int32, (8, 1), 0)
    blk = lax.broadcasted_iota(jnp.int32, (1, nk), 1)
    cend = jnp.sum(jnp.where(sub == h, cend_ref[0], 0.0), axis=0, keepdims=True)
    end_of = lambda b: jnp.sum(jnp.where(blk == b, cend, 0.0), axis=1, keepdims=True)
    c_ref = end_of(diag - 1)

    amax = lambda r: jnp.max(jnp.abs(r[...]), axis=1, keepdims=True)
    qk_bound = 8.0 * BF16_MARGIN * amax(qg_ref) * amax(kg_ref)
    needed = (blk < diag) & (c_ref - cend > -(2.0 * qk_bound + UNDERFLOW_CUT))
    first = diag - jnp.sum(jnp.where(needed, 1, 0))

    def block(j, width, carry, diagonal):
        m, acc = carry
        start = pl.multiple_of(j * tk, tk)
        z = lax.dot_general(k_ref[pl.ds(start, width * tk), :], q, (((1,), (1,)), ((), ())),
                            preferred_element_type=F32)
        if diagonal:
            key = lax.broadcasted_iota(jnp.int32, z.shape, 0)
            qry = lax.broadcasted_iota(jnp.int32, z.shape, 1)
            z = jnp.where(key <= qry, z, -1e30)
        for u in range(width):
            z_u = z[u * tk:(u + 1) * tk, :]
            s_u = (end_of(j + u - 1) - c_ref) * LOG2E
            m_new = jnp.maximum(m, jnp.max(z_u, axis=0, keepdims=True) - s_u)
            p_u = jnp.exp2(z_u - (m_new + s_u)).astype(BF16)
            acc = acc * jnp.exp2(m - m_new) + jnp.dot(vt_ref[j + u], p_u, preferred_element_type=F32)
            m = m_new
        return m, acc

    carry = (jnp.full((1, tq), -1e30, F32), jnp.zeros((PAIR, tq), F32))
    carry = block(diag, q_blocks, carry, True)
    i = diag
    wide = jnp.minimum((i - first + FOX_WIDE - 1) // FOX_WIDE, i // FOX_WIDE)
    lo = i - wide * FOX_WIDE
    carry = lax.fori_loop(0, wide, lambda g, c: block(lo + g * FOX_WIDE, FOX_WIDE, c, False), carry)
    carry = lax.fori_loop(first, lo, lambda j, c: block(j, 1, c, False), carry)
    acc = carry[1]
    value_row = lax.broadcasted_iota(jnp.int32, (PAIR, 1), 0) < HEAD_DIM
    out_t = jnp.where(value_row, acc * (1.0 / acc[HEAD_DIM:HEAD_DIM + 1, :]), 0.0)
    o_ref[...] = out_t.T


def _fox_attn(seq_len, q, k, vt, cend, qg, kg, tk, q_blocks):
    n = q.shape[0]
    tq = tk * q_blocks
    nq, nk = seq_len // tq, seq_len // tk
    full = lambda a: pl.BlockSpec(a.shape, lambda bi, h, i: (0,) * a.ndim)
    return pl.pallas_call(
        _fox_attn_body,
        grid=(n // seq_len, N_HEADS, nq),
        in_specs=[pl.BlockSpec((tq, PAIR), lambda bi, h, i: (bi * nq + i, h)),
                  pl.BlockSpec((seq_len, PAIR), lambda bi, h, i: (bi, h)),
                  pl.BlockSpec((nk, PAIR, tk), lambda bi, h, i: (bi, h, 0)),
                  pl.BlockSpec((1, 8, nk), lambda bi, h, i: (bi, 0, 0)),
                  full(qg), full(kg)],
        out_specs=pl.BlockSpec((tq, PAIR), lambda bi, h, i: (bi * nq + i, h)),
        out_shape=jax.ShapeDtypeStruct((n, N_HEADS * PAIR), F32),
        compiler_params=_params("parallel", "parallel", "arbitrary"),
        name="fox_attn",
    )(q, k, vt, cend, qg, kg)


def _mix_out_body(seq_len, x_ref, ys_ref, g_ref, bonus_ref, at_ref, fg_ref, pool_ref, halo_ref,
                  ln_ref, pw_ref, ps_ref, wo_ref, pg_ref, o_ref):
    tm = x_ref.shape[0]
    row0 = lax.rem(pl.program_id(0) * tm, seq_len)

    y = ys_ref[...]
    d = y - _head_sums(y) * (1.0 / HEAD_DIM)
    var = _head_sums(d * d) * (1.0 / HEAD_DIM)
    y_rwkv = (d * lax.rsqrt(var + GN_EPS) * ln_ref[0:1, :] + ln_ref[1:2, :] + bonus_ref[...]) * g_ref[...]

    left = lax.broadcasted_iota(jnp.int32, (1, PAIR), 1) < HEAD_DIM
    at = at_ref[...]
    packed = [jnp.where(left, at[:, 2 * p * PAIR:(2 * p + 1) * PAIR],
                        pltpu.roll(at[:, (2 * p + 1) * PAIR:(2 * p + 2) * PAIR], HEAD_DIM, axis=1))
              for p in range(N_PAIRS)]
    y_fox = jnp.concatenate(packed, axis=1) * _sigmoid(fg_ref[...])

    pin = pool_ref[...]
    halo = jnp.where(row0 == 0, 0.0, halo_ref[...])
    s = jnp.concatenate([halo, pin], axis=0)
    sums = []
    for shift in (1, 2, 4, 8):
        s = s + pltpu.roll(s, shift, axis=0)
        sums.append(s[POOL_HALO:, :])
    grp = lax.broadcasted_iota(jnp.int32, (1, POOL_W), 1) // HEAD_DIM
    win_sum = jnp.where(grp == 0, sums[0], jnp.where(grp == 1, sums[1], jnp.where(grp == 2, sums[2], sums[3])))
    win = jnp.where(grp == 0, 2.0, jnp.where(grp == 1, 4.0, jnp.where(grp == 2, 8.0, 16.0)))
    pos = (row0 + lax.broadcasted_iota(jnp.int32, (tm, 1), 0) + 1).astype(F32)
    u = win_sum / jnp.minimum(pos, win) - pin
    y_pool = _dot(u, pw_ref[...]) * ps_ref[...]

    mixed = (_dot(y_rwkv, wo_ref[0:384, :]) + _dot(y_fox, wo_ref[384:768, :])
             + _dot(y_pool, wo_ref[768:1024, :]))
    o_ref[...] = x_ref[...] + _rms(mixed, pg_ref[...])


def _mix_out(seq_len, x, y_scan, g, bonus, attn, fox, pool_in, ln, pw, ps, wo, pg, tm=256):
    n = x.shape[0]
    full = lambda a: pl.BlockSpec(a.shape, lambda i: (0,) * a.ndim)
    tile = pl.BlockSpec((tm, MIX_W), lambda i: (i, 0))
    return pl.pallas_call(
        functools.partial(_mix_out_body, seq_len),
        grid=(n // tm,),
        in_specs=[pl.BlockSpec((tm, D_MODEL), lambda i: (i, 0)), tile, tile, tile,
                  pl.BlockSpec((tm, N_HEADS * PAIR), lambda i: (i, 0)),
                  pl.BlockSpec((tm, MIX_W), lambda i: (i, 3)),
                  pl.BlockSpec((tm, POOL_W), lambda i: (i, 0)),
                  pl.BlockSpec((POOL_HALO, POOL_W),
                               lambda i: (jnp.maximum(i * (tm // POOL_HALO) - 1, 0), 0)),
                  full(ln), full(pw), full(ps), full(wo), full(pg)],
        out_specs=pl.BlockSpec((tm, D_MODEL), lambda i: (i, 0)),
        out_shape=jax.ShapeDtypeStruct((n, D_MODEL), F32),
        compiler_params=_params("parallel"),
        name="mix_out",
    )(x, y_scan, g, bonus, attn, fox, pool_in, pool_in, ln, pw, ps, wo, pg)


def _ffn_body(x_ref, g1_ref, wgu_ref, wd_ref, g2_ref, o_ref):
    x = x_ref[...]
    h = _rms(x, g1_ref[...]).astype(BF16)
    acc = jnp.zeros(x.shape, F32)
    for c in range(FFN_HIDDEN // FFN_CHUNK):
        c0 = c * FFN_CHUNK
        gate = jnp.dot(h, wgu_ref[:, c0:c0 + FFN_CHUNK], preferred_element_type=F32)
        up = jnp.dot(h, wgu_ref[:, FFN_HIDDEN + c0:FFN_HIDDEN + c0 + FFN_CHUNK],
                     preferred_element_type=F32)
        act = (gate * _sigmoid(gate) * up).astype(BF16)
        acc = acc + jnp.dot(act, wd_ref[c0:c0 + FFN_CHUNK, :], preferred_element_type=F32)
    o_ref[...] = x + _rms(acc, g2_ref[...])


def _ffn(x, g1, wgu, wd, g2, tm=512):
    n = x.shape[0]
    full = lambda a: pl.BlockSpec(a.shape, lambda i: (0,) * a.ndim)
    tile = pl.BlockSpec((tm, D_MODEL), lambda i: (i, 0))
    return pl.pallas_call(
        _ffn_body,
        grid=(n // tm,),
        in_specs=[tile, full(g1), full(wgu), full(wd), full(g2)],
        out_specs=tile,
        out_shape=jax.ShapeDtypeStruct((n, D_MODEL), F32),
        compiler_params=_params("parallel"),
        name="ffn",
    )(x, g1, wgu, wd, g2)


def _pad_rows(a, rows, at=0):
    out = jnp.zeros((rows, a.shape[1]), a.dtype)
    return lax.dynamic_update_slice(out, a, (at, 0))


def kernel(x, mix_pre_g, mix_post_g, ffn_pre_g, ffn_post_g, w_in, w_out, rwkv_mu, rwkv_w0, rwkv_w2,
           rwkv_a0, rwkv_a2, rwkv_g2, rwkv_v0, rwkv_v1, rwkv_v2, rwkv_k_k, rwkv_k_a, rwkv_r_k,
           rwkv_ln_w, rwkv_ln_b, fox_q_g, fox_k_g, fox_f_b, pool_w, pool_scale, ffn_w_gu, ffn_w_down):
    batch, seq_len, _ = x.shape
    n = batch * seq_len
    depth = w_in.shape[0]
    row = lambda a: a.reshape(1, -1).astype(F32)
    xf = x.reshape(n, D_MODEL)
    w_arr = _w_regroup(w_in)
    v_first = None
    for l in range(depth):
        feats, fox, ff, pool_in = _inproj(xf, row(mix_pre_g[l]), w_arr[l])

        mu = jnp.pad(rwkv_mu[l], (0, RWKV_PAD - RWKV_IN)).reshape(1, RWKV_PAD)
        v0 = rwkv_v0[l - 1] if l > 0 else jnp.zeros((MIX_W,), F32)
        vecs = jnp.stack([rwkv_w0[l], rwkv_a0[l], rwkv_k_k[l], rwkv_k_a[l], rwkv_r_k[l].reshape(-1), v0,
                          jnp.zeros((MIX_W,), F32), jnp.zeros((MIX_W,), F32)])
        w2p = _pad_rows(rwkv_w2[l], 128, 0).astype(BF16)
        a2p = _pad_rows(rwkv_a2[l], 128, 64).astype(BF16)
        g2p = _pad_rows(rwkv_g2[l], 256, 0).astype(BF16)
        vres = None
        if l > 0:
            v1p = jnp.pad(rwkv_v1[l - 1], ((0, 0), (0, 128 - 32))).astype(BF16)
            v2p = _pad_rows(rwkv_v2[l - 1], 128, 0).astype(BF16)
            vres = (v_first, v1p, v2p)
        r, k, v, lw, kk, b, g, bonus = _rwkv_prep(seq_len, feats, mu, vecs, w2p, a2p, g2p, vres)
        if l == 0:
            v_first = v
        y_scan = _rwkv_scan(seq_len, r, lw, k, v, kk, b)

        tile6 = lambda a: jnp.tile(a, N_HEADS).reshape(1, MIX_W)
        fb = jnp.pad(fox_f_b[l], (0, FF_PAD - N_HEADS)).reshape(1, FF_PAD)
        qg, kg = tile6(fox_q_g[l]), tile6(fox_k_g[l])
        q_a, k_a, v_t, cend = _fox_prep(seq_len, fox, ff, qg, kg, fb, FOX_BLOCK)
        attn = _fox_attn(seq_len, q_a, k_a, v_t, cend, qg, kg, FOX_BLOCK, FOX_Q_BLOCKS)

        ln = jnp.stack([rwkv_ln_w[l], rwkv_ln_b[l]] + [jnp.zeros((MIX_W,), F32)] * 6)
        pw = jax.scipy.linalg.block_diag(*[pool_w[l, gi] for gi in range(4)]).astype(BF16)
        xf = _mix_out(seq_len, xf, y_scan, g, bonus, attn, fox, pool_in, ln, pw,
                      row(pool_scale[l]), w_out[l].astype(BF16), row(mix_post_g[l]))
        xf = _ffn(xf, row(ffn_pre_g[l]), ffn_w_gu[l].astype(BF16), ffn_w_down[l].astype(BF16),
                  row(ffn_post_g[l]))
    return xf.reshape(batch, seq_len, D_MODEL)
```

```python
import functools

import jax
import jax.numpy as jnp
from jax import lax
from jax.experimental import pallas as pl
from jax.experimental.pallas import tpu as pltpu

F32 = jnp.float32
BF16 = jnp.bfloat16

D_MODEL = 1024
HEAD_DIM = 64
PAIR = 2 * HEAD_DIM
N_HEADS = 6
N_PAIRS = N_HEADS // 2
MIX_W = N_HEADS * HEAD_DIM
POOL_W = 256
POOL_HALO = 16
FFN_HIDDEN = 2816
FFN_CHUNK = 256
RMS_EPS = 1e-6
GN_EPS = HEAD_DIM * 1e-5

RWKV_IN = 1440
RWKV_PAD = 1536
FOX_IN = 4 * MIX_W
FF_PAD = 128
IN_SPLITS = ((0, 1536), (1536, 3072), (3072, 3200), (3200, 3456))
IN_TOTAL = 3456

LOG2E = 1.4426950408889634
FOX_BLOCK = 256
FOX_Q_BLOCKS = 2
FOX_WIDE = 4
UNDERFLOW_CUT = 110.0
BF16_MARGIN = 1.02
SCAN_CHUNK = 64
VMEM_LIMIT = 56 * 1024 * 1024


def _params(*sem):
    return pltpu.CompilerParams(dimension_semantics=sem, vmem_limit_bytes=VMEM_LIMIT)


def _dot(a, b):
    return jnp.dot(a.astype(BF16), b.astype(BF16), preferred_element_type=F32)


def _dot_nt(a, b):
    return lax.dot_general(a.astype(BF16), b.astype(BF16), (((1,), (1,)), ((), ())),
                           preferred_element_type=F32)


def _split3(x):
    hi = x.astype(BF16)
    r1 = x - hi.astype(F32)
    mid = r1.astype(BF16)
    lo = (r1 - mid.astype(F32)).astype(BF16)
    return hi, mid, lo


def _head_sums(x):
    left = lax.broadcasted_iota(jnp.int32, (1, PAIR), 1) < HEAD_DIM
    outs = []
    for p in range(x.shape[1] // PAIR):
        xp = x[:, p * PAIR:(p + 1) * PAIR]
        lsum = jnp.sum(jnp.where(left, xp, 0.0), axis=1, keepdims=True)
        rsum = jnp.sum(jnp.where(left, 0.0, xp), axis=1, keepdims=True)
        outs.append(jnp.where(left, lsum, rsum))
    return jnp.concatenate(outs, axis=1)


def _dot3r(w, x):
    hi, mid, lo = _split3(x)
    d = lambda t: jnp.dot(w, t, preferred_element_type=F32)
    return d(hi) + d(mid) + d(lo)


def _rms(x, g):
    ms = jnp.mean(x * x, axis=-1, keepdims=True)
    return x * lax.rsqrt(ms + RMS_EPS) * g


def _softplus(z):
    return jnp.maximum(z, 0.0) + jnp.log1p(jnp.exp(-jnp.abs(z)))


def _sigmoid(z):
    return 1.0 / (1.0 + jnp.exp(-z))


def _tri(n, inclusive, dtype):
    r = lax.broadcasted_iota(jnp.int32, (n, n), 0)
    c = lax.broadcasted_iota(jnp.int32, (n, n), 1)
    keep = (r >= c) if inclusive else (r > c)
    return jnp.where(keep, 1.0, 0.0).astype(dtype)


def _w_regroup_body(w_ref, o_ref):
    w = w_ref[0]
    zeros = lambda c: jnp.zeros((w.shape[0], c), F32)
    fox0, ff0, pool0 = RWKV_IN, RWKV_IN + FOX_IN, RWKV_IN + FOX_IN + N_HEADS
    o_ref[0] = jnp.concatenate(
        [w[:, :fox0], zeros(RWKV_PAD - RWKV_IN), w[:, fox0:ff0], w[:, ff0:pool0],
         zeros(FF_PAD - N_HEADS), w[:, pool0:]], axis=1).astype(BF16)


def _w_regroup(w_in, rows=128):
    depth, d, cols = w_in.shape
    return pl.pallas_call(
        _w_regroup_body,
        grid=(depth, d // rows),
        in_specs=[pl.BlockSpec((1, rows, cols), lambda l, i: (l, i, 0))],
        out_specs=pl.BlockSpec((1, rows, IN_TOTAL), lambda l, i: (l, i, 0)),
        out_shape=jax.ShapeDtypeStruct((depth, d, IN_TOTAL), BF16),
        compiler_params=_params("parallel", "parallel"),
        name="w_regroup",
    )(w_in)


def _rwkv_features(f, halo, mu, vecs, w2, a2, g2, vres):
    tm = f.shape[0]
    row = lax.broadcasted_iota(jnp.int32, (tm, 1), 0)
    prev = jnp.where(row == 0, halo, pltpu.roll(f, 1, axis=0))
    f = f + (prev - f) * mu
    r, k, v = f[:, 0:384], f[:, 384:768], f[:, 768:1152]
    wa = f[:, 1152:1280]
    gd = f[:, 1280:1536]
    w0, a0, k_k, k_a, r_k, v0 = (vecs[j:j + 1, :] for j in range(6))

    w_log = -_softplus(-(w0 + _dot(jnp.tanh(wa), w2))) - 0.5
    lw = -jnp.exp(w_log)
    a = _sigmoid(a0 + _dot(wa, a2))
    g = _dot(_sigmoid(gd), g2)
    if vres is not None:
        v_first, v1, v2 = vres
        v = v + (v_first - v) * _sigmoid(v0 + _dot(_dot(v, v1), v2))
    kk = k * k_k
    kk = kk / jnp.maximum(jnp.sqrt(_head_sums(kk * kk)), 1e-12)
    k = k * (1.0 + (a - 1.0) * k_a)
    return r, k, v, lw, kk, kk * a, g, _head_sums(r * k * r_k) * v


def _scan_body(chunks_per_step, r_ref, lw_ref, k_ref, v_ref, kk_ref, b_ref, y_ref, h_ref):
    c_len = SCAN_CHUNK
    n_batch = r_ref.shape[0]

    @pl.when(pl.program_id(0) == 0)
    def _():
        h_ref[...] = jnp.zeros_like(h_ref)

    lane = lax.broadcasted_iota(jnp.int32, (1, PAIR), 1)
    left = lane < HEAD_DIM
    ri = lax.broadcasted_iota(jnp.int32, (PAIR, PAIR), 0)
    ci = lax.broadcasted_iota(jnp.int32, (PAIR, PAIR), 1)
    strict, incl, diag = ri > ci, ri >= ci, ri == ci
    tri = _tri(c_len, True, BF16)
    eye = jnp.where(diag, 1.0, 0.0)

    def stack(x):
        return jnp.concatenate([jnp.where(left, x, 0.0), jnp.where(left, 0.0, x)], axis=0)

    seqs = [(bi, p) for bi in range(n_batch) for p in range(N_PAIRS)]
    items = [(bi, p, c) for (bi, p) in seqs for c in range(chunks_per_step)]
    window = lambda bi, p, c: (bi, slice(c * c_len, (c + 1) * c_len), slice(p * PAIR, (p + 1) * PAIR))
    load = lambda ref: [ref[window(*it)] for it in items]
    each = lambda f, *cols: [f(*xs) for xs in zip(*cols)]

    lw = load(lw_ref)
    cum = each(lambda x: _dot3r(tri, x), lw)
    cum_end = each(lambda x: x[c_len - 1:c_len, :], cum)
    e_neg = each(lambda x: jnp.exp(-x), cum)
    e_tail = each(lambda x, xe: jnp.exp(xe - x), cum, cum_end)
    kk, b, k = load(kk_ref), load(b_ref), load(k_ref)
    l_a = each(lambda x, c_, w_: stack(-x * jnp.exp(c_ - w_)), kk, cum, lw)
    l_r = each(lambda x, c_: stack(x * jnp.exp(c_)), load(r_ref), cum)
    r_b = each(lambda x, e: stack(x * e), b, e_neg)
    r_k = each(lambda x, e: stack(x * e), k, e_neg)
    rb_t = each(lambda x, e: stack(x * e).T, b, e_tail)
    rk_t = each(lambda x, e: stack(x * e).T, k, e_tail)
    v_s = each(stack, load(v_ref))

    a_ab = each(lambda x, y: jnp.where(strict, _dot_nt(x, y), 0.0), l_a, r_b)
    a_ak = each(lambda x, y: jnp.where(strict, _dot_nt(x, y), 0.0), l_a, r_k)
    a_rb = each(lambda x, y: jnp.where(incl, _dot_nt(x, y), 0.0), l_r, r_b)
    a_rk = each(lambda x, y: jnp.where(incl, _dot_nt(x, y), 0.0), l_r, r_k)

    x = a_ab
    t_inv = each(lambda a: eye + a, x)
    for _ in range(5):
        x = each(lambda a: _dot(a, a), x)
        t_inv = each(lambda t, a: t + _dot(t, a), t_inv, x)

    t_a = each(_dot, t_inv, l_a)
    w_ = each(lambda t, a, v: _dot(t, _dot(a, v)), t_inv, a_ak, v_s)
    r_q = each(lambda l, a, t: l + _dot(a, t), l_r, a_rb, t_a)
    y_0 = each(lambda a, w, a2, v: _dot(a, w) + _dot(a2, v), a_rb, w_, a_rk, v_s)
    m_ = each(lambda ce, bt, t: jnp.where(diag, jnp.exp(ce), 0.0) + _dot(bt, t), cum_end, rb_t, t_a)
    n_ = each(lambda bt, w, kt, v: _dot(bt, w) + _dot(kt, v), rb_t, w_, rk_t, v_s)

    for s, (bi, p) in enumerate(seqs):
        h = h_ref[s]
        for c in range(chunks_per_step):
            idx = s * chunks_per_step + c
            y_st = _dot(r_q[idx], h) + y_0[idx]
            y_ref[window(bi, p, c)] = y_st[:c_len] + y_st[c_len:]
            h = _dot(m_[idx], h) + n_[idx]
        h_ref[s] = h


def _rwkv_scan(seq_len, r, lw, k, v, kk, b, chunks_per_step=2):
    n = r.shape[0]
    n_batch = n // seq_len
    rows = chunks_per_step * SCAN_CHUNK
    spec = pl.BlockSpec((n_batch, rows, MIX_W), lambda j: (0, j, 0))
    args = [a.reshape(n_batch, seq_len, MIX_W) for a in (r, lw, k, v, kk, b)]
    y = pl.pallas_call(
        functools.partial(_scan_body, chunks_per_step),
        grid=(seq_len // rows,),
        in_specs=[spec] * 6,
        out_specs=spec,
        out_shape=jax.ShapeDtypeStruct((n_batch, seq_len, MIX_W), F32),
        scratch_shapes=[pltpu.VMEM((n_batch * N_PAIRS, PAIR, PAIR), F32)],
        compiler_params=_params("arbitrary"),
        name="rwkv_scan",
    )(*args)
    return y.reshape(n, MIX_W)


def _fox_features(q, k, v, ff, qg, kg, fb):
    tm = q.shape[0]

    def qk_norm(x, g):
        ms = _head_sums(x * x) * (1.0 / HEAD_DIM)
        return x * lax.rsqrt(ms + RMS_EPS) * g

    lane = lax.broadcasted_iota(jnp.int32, (1, PAIR), 1)
    left = lane < HEAD_DIM

    def pad_heads(x, aug):
        cols = []
        for h in range(N_HEADS):
            src = x[:, (h // 2) * PAIR:(h // 2 + 1) * PAIR]
            if h % 2:
                src = pltpu.roll(src, HEAD_DIM, axis=1)
            cols.append(jnp.where(left, src, aug(h)))
        return jnp.concatenate(cols, axis=1).astype(BF16)

    log_f = -_softplus(-(ff + fb))
    local = _dot3r(_tri(tm, True, BF16), log_f)
    bias = local * (-LOG2E)

    def k_aug(h):
        col = jnp.sum(jnp.where(lane == h, bias, 0.0), axis=1, keepdims=True)
        hi, mid, lo = (p.astype(F32) for p in _split3(col))
        return jnp.where(lane == HEAD_DIM, hi,
                         jnp.where(lane == HEAD_DIM + 1, mid, jnp.where(lane == HEAD_DIM + 2, lo, 0.0)))

    q_aug = jnp.where((lane >= HEAD_DIM) & (lane < HEAD_DIM + 3), 1.0, 0.0)
    q_a = pad_heads(qk_norm(q, qg) * (LOG2E * HEAD_DIM ** -0.5), lambda h: q_aug)
    k_a = pad_heads(qk_norm(k, kg), k_aug)

    v_t = v.T
    ones_row = jnp.where(lax.broadcasted_iota(jnp.int32, (HEAD_DIM, tm), 0) == 0, 1.0, 0.0)
    rows = []
    for h in range(N_HEADS):
        rows += [v_t[h * HEAD_DIM:(h + 1) * HEAD_DIM, :], ones_row]
    return q_a, k_a, jnp.concatenate(rows, axis=0).astype(BF16), local


def _proj_body(seq_len, has_vres, *refs):
    n_in = 14 if has_vres else 11
    x_ref, g_ref, w_ref, mu_ref, vec_ref, w2_ref, a2_ref, g2_ref, qg_ref, kg_ref, fb_ref = refs[:11]
    vres_refs = refs[11:14] if has_vres else None
    (r_o, k_o, v_o, lw_o, kk_o, b_o, g_o, bonus_o,
     qa_o, ka_o, vt_o, cend_o, fg_o, pool_o) = refs[n_in:n_in + 14]
    prev_ref, carry_ref = refs[n_in + 14:]
    tm = x_ref.shape[0]
    blocks = seq_len // tm
    j = lax.rem(pl.program_id(0), blocks)

    @pl.when(j == 0)
    def _():
        prev_ref[...] = jnp.zeros_like(prev_ref)
        carry_ref[...] = jnp.zeros_like(carry_ref)
        cend_o[...] = jnp.zeros_like(cend_o)

    h = _rms(x_ref[...], g_ref[...]).astype(BF16)
    feats, fox, ff, pool = (jnp.dot(h, w_ref[:, c0:c1], preferred_element_type=F32) for c0, c1 in IN_SPLITS)
    pool_o[...] = pool
    fg_o[...] = fox[:, 3 * MIX_W:]

    halo = prev_ref[7:8, :]
    prev_ref[...] = feats[tm - 8:, :]
    vres = None if vres_refs is None else tuple(ref[...] for ref in vres_refs)
    outs = _rwkv_features(feats, halo, mu_ref[...], vec_ref[...], w2_ref[...], a2_ref[...], g2_ref[...], vres)
    for o_ref, val in zip((r_o, k_o, v_o, lw_o, kk_o, b_o, g_o, bonus_o), outs):
        o_ref[...] = val

    q_a, k_a, v_t, local = _fox_features(fox[:, :MIX_W], fox[:, MIX_W:2 * MIX_W], fox[:, 2 * MIX_W:3 * MIX_W],
                                         ff, qg_ref[...], kg_ref[...], fb_ref[...])
    qa_o[...] = q_a
    ka_o[...] = k_a
    vt_o[0] = v_t
    cum = local + carry_ref[...]
    carry_ref[...] = cum[tm - 1:tm, :]
    blk = lax.broadcasted_iota(jnp.int32, (1, blocks), 1)
    cend_o[0] = jnp.where(blk == j, cum.T[0:8, tm - 1:tm], cend_o[0])


def _proj(seq_len, x, g, w, mu, vecs, w2p, a2p, g2p, qg, kg, fb, vres, tm):
    n = x.shape[0]
    blocks = seq_len // tm
    full = lambda a: pl.BlockSpec(a.shape, lambda i: (0,) * a.ndim)
    tile = lambda c: pl.BlockSpec((tm, c), lambda i: (i, 0))
    in_specs = [tile(D_MODEL)] + [full(a) for a in (g, w, mu, vecs, w2p, a2p, g2p, qg, kg, fb)]
    args = [x, g, w, mu, vecs, w2p, a2p, g2p, qg, kg, fb]
    if vres is not None:
        v_first, v1p, v2p = vres
        in_specs += [tile(MIX_W), full(v1p), full(v2p)]
        args += [v_first, v1p, v2p]
    padded = N_HEADS * PAIR
    return pl.pallas_call(
        functools.partial(_proj_body, seq_len, vres is not None),
        grid=(n // tm,),
        in_specs=in_specs,
        out_specs=[tile(MIX_W)] * 8 + [tile(padded), tile(padded),
                                       pl.BlockSpec((1, padded, tm), lambda i: (i, 0, 0)),
                                       pl.BlockSpec((1, 8, blocks), lambda i: (i // blocks, 0, 0)),
                                       tile(MIX_W), tile(POOL_W)],
        out_shape=[jax.ShapeDtypeStruct((n, MIX_W), F32)] * 8
        + [jax.ShapeDtypeStruct((n, padded), BF16)] * 2
        + [jax.ShapeDtypeStruct((n // tm, padded, tm), BF16),
           jax.ShapeDtypeStruct((n // seq_len, 8, blocks), F32),
           jax.ShapeDtypeStruct((n, MIX_W), F32), jax.ShapeDtypeStruct((n, POOL_W), F32)],
        scratch_shapes=[pltpu.VMEM((8, RWKV_PAD), F32), pltpu.VMEM((1, FF_PAD), F32)],
        compiler_params=_params("arbitrary"),
        name="proj",
    )(*args)


def _fox_attn_body(q_ref, k_ref, vt_ref, cend_ref, qg_ref, kg_ref, o_ref):
    h = pl.program_id(1)
    tq = q_ref.shape[0]
    tk = vt_ref.shape[2]
    q_blocks = tq // tk
    diag = pl.program_id(2) * q_blocks
    nk = cend_ref.shape[2]
    q = q_ref[...]
    sub = lax.broadcasted_iota(jnp.int32, (8, 1), 0)
    blk = lax.broadcasted_iota(jnp.int32, (1, nk), 1)
    cend = jnp.sum(jnp.where(sub == h, cend_ref[0], 0.0), axis=0, keepdims=True)
    end_of = lambda b: jnp.sum(jnp.where(blk == b, cend, 0.0), axis=1, keepdims=True)
    c_ref = end_of(diag - 1)

    amax = lambda r: jnp.max(jnp.abs(r[...]), axis=1, keepdims=True)
    qk_bound = 8.0 * BF16_MARGIN * amax(qg_ref) * amax(kg_ref)
    needed = (blk < diag) & (c_ref - cend > -(2.0 * qk_bound + UNDERFLOW_CUT))
    first = diag - jnp.sum(jnp.where(needed, 1, 0))

    def block(j, width, carry, diagonal):
        m, acc = carry
        start = pl.multiple_of(j * tk, tk)
        z = lax.dot_general(k_ref[pl.ds(start, width * tk), :], q, (((1,), (1,)), ((), ())),
                            preferred_element_type=F32)
        if diagonal:
            key = lax.broadcasted_iota(jnp.int32, z.shape, 0)
            qry = lax.broadcasted_iota(jnp.int32, z.shape, 1)
            z = jnp.where(key <= qry, z, -1e30)
        for u in range(width):
            z_u = z[u * tk:(u + 1) * tk, :]
            s_u = (end_of(j + u - 1) - c_ref) * LOG2E
            m_new = jnp.maximum(m, jnp.max(z_u, axis=0, keepdims=True) - s_u)
            p_u = jnp.exp2(z_u - (m_new + s_u)).astype(BF16)
            acc = acc * jnp.exp2(m - m_new) + jnp.dot(vt_ref[j + u], p_u, preferred_element_type=F32)
            m = m_new
        return m, acc

    carry = (jnp.full((1, tq), -1e30, F32), jnp.zeros((PAIR, tq), F32))
    carry = block(diag, q_blocks, carry, True)
    i = diag
    wide = jnp.minimum((i - first + FOX_WIDE - 1) // FOX_WIDE, i // FOX_WIDE)
    lo = i - wide * FOX_WIDE
    carry = lax.fori_loop(0, wide, lambda g, c: block(lo + g * FOX_WIDE, FOX_WIDE, c, False), carry)
    carry = lax.fori_loop(first, lo, lambda j, c: block(j, 1, c, False), carry)
    acc = carry[1]
    value_row = lax.broadcasted_iota(jnp.int32, (PAIR, 1), 0) < HEAD_DIM
    out_t = jnp.where(value_row, acc * (1.0 / acc[HEAD_DIM:HEAD_DIM + 1, :]), 0.0)
    o_ref[...] = out_t.T


def _fox_attn(seq_len, q, k, vt, cend, qg, kg, tk, q_blocks):
    n = q.shape[0]
    tq = tk * q_blocks
    nq, nk = seq_len // tq, seq_len // tk
    full = lambda a: pl.BlockSpec(a.shape, lambda bi, h, i: (0,) * a.ndim)
    return pl.pallas_call(
        _fox_attn_body,
        grid=(n // seq_len, N_HEADS, nq),
        in_specs=[pl.BlockSpec((tq, PAIR), lambda bi, h, i: (bi * nq + i, h)),
                  pl.BlockSpec((seq_len, PAIR), lambda bi, h, i: (bi, h)),
                  pl.BlockSpec((nk, PAIR, tk), lambda bi, h, i: (bi, h, 0)),
                  pl.BlockSpec((1, 8, nk), lambda bi, h, i: (bi, 0, 0)),
                  full(qg), full(kg)],
        out_specs=pl.BlockSpec((tq, PAIR), lambda bi, h, i: (bi * nq + i, h)),
        out_shape=jax.ShapeDtypeStruct((n, N_HEADS * PAIR), F32),
        compiler_params=_params("parallel", "parallel", "arbitrary"),
        name="fox_attn",
    )(q, k, vt, cend, qg, kg)


def _mix_out_body(seq_len, x_ref, ys_ref, g_ref, bonus_ref, at_ref, fg_ref, pool_ref, halo_ref,
                  ln_ref, pw_ref, ps_ref, wo_ref, pg_ref, o_ref):
    tm = x_ref.shape[0]
    row0 = lax.rem(pl.program_id(0) * tm, seq_len)

    y = ys_ref[...]
    d = y - _head_sums(y) * (1.0 / HEAD_DIM)
    var = _head_sums(d * d) * (1.0 / HEAD_DIM)
    y_rwkv = (d * lax.rsqrt(var + GN_EPS) * ln_ref[0:1, :] + ln_ref[1:2, :] + bonus_ref[...]) * g_ref[...]

    left = lax.broadcasted_iota(jnp.int32, (1, PAIR), 1) < HEAD_DIM
    at = at_ref[...]
    packed = [jnp.where(left, at[:, 2 * p * PAIR:(2 * p + 1) * PAIR],
                        pltpu.roll(at[:, (2 * p + 1) * PAIR:(2 * p + 2) * PAIR], HEAD_DIM, axis=1))
              for p in range(N_PAIRS)]
    y_fox = jnp.concatenate(packed, axis=1) * _sigmoid(fg_ref[...])

    pin = pool_ref[...]
    halo = jnp.where(row0 == 0, 0.0, halo_ref[...])
    s = jnp.concatenate([halo, pin], axis=0)
    sums = []
    for shift in (1, 2, 4, 8):
        s = s + pltpu.roll(s, shift, axis=0)
        sums.append(s[POOL_HALO:, :])
    grp = lax.broadcasted_iota(jnp.int32, (1, POOL_W), 1) // HEAD_DIM
    win_sum = jnp.where(grp == 0, sums[0], jnp.where(grp == 1, sums[1], jnp.where(grp == 2, sums[2], sums[3])))
    win = jnp.where(grp == 0, 2.0, jnp.where(grp == 1, 4.0, jnp.where(grp == 2, 8.0, 16.0)))
    pos = (row0 + lax.broadcasted_iota(jnp.int32, (tm, 1), 0) + 1).astype(F32)
    u = win_sum / jnp.minimum(pos, win) - pin
    y_pool = _dot(u, pw_ref[...]) * ps_ref[...]

    mixed = (_dot(y_rwkv, wo_ref[0:384, :]) + _dot(y_fox, wo_ref[384:768, :])
             + _dot(y_pool, wo_ref[768:1024, :]))
    o_ref[...] = x_ref[...] + _rms(mixed, pg_ref[...])


def _mix_out(seq_len, x, y_scan, g, bonus, attn, fg, pool_in, ln, pw, ps, wo, pg, tm=256):
    n = x.shape[0]
    full = lambda a: pl.BlockSpec(a.shape, lambda i: (0,) * a.ndim)
    tile = pl.BlockSpec((tm, MIX_W), lambda i: (i, 0))
    return pl.pallas_call(
        functools.partial(_mix_out_body, seq_len),
        grid=(n // tm,),
        in_specs=[pl.BlockSpec((tm, D_MODEL), lambda i: (i, 0)), tile, tile, tile,
                  pl.BlockSpec((tm, N_HEADS * PAIR), lambda i: (i, 0)), tile,
                  pl.BlockSpec((tm, POOL_W), lambda i: (i, 0)),
                  pl.BlockSpec((POOL_HALO, POOL_W),
                               lambda i: (jnp.maximum(i * (tm // POOL_HALO) - 1, 0), 0)),
                  full(ln), full(pw), full(ps), full(wo), full(pg)],
        out_specs=pl.BlockSpec((tm, D_MODEL), lambda i: (i, 0)),
        out_shape=jax.ShapeDtypeStruct((n, D_MODEL), F32),
        compiler_params=_params("parallel"),
        name="mix_out",
    )(x, y_scan, g, bonus, attn, fg, pool_in, pool_in, ln, pw, ps, wo, pg)


def _ffn_body(x_ref, g1_ref, wgu_ref, wd_ref, g2_ref, o_ref):
    x = x_ref[...]
    h = _rms(x, g1_ref[...]).astype(BF16)
    acc = jnp.zeros(x.shape, F32)
    for c in range(FFN_HIDDEN // FFN_CHUNK):
        c0 = c * FFN_CHUNK
        gate = jnp.dot(h, wgu_ref[:, c0:c0 + FFN_CHUNK], preferred_element_type=F32)
        up = jnp.dot(h, wgu_ref[:, FFN_HIDDEN + c0:FFN_HIDDEN + c0 + FFN_CHUNK],
                     preferred_element_type=F32)
        act = (gate * _sigmoid(gate) * up).astype(BF16)
        acc = acc + jnp.dot(act, wd_ref[c0:c0 + FFN_CHUNK, :], preferred_element_type=F32)
    o_ref[...] = x + _rms(acc, g2_ref[...])


def _ffn(x, g1, wgu, wd, g2, tm=512):
    n = x.shape[0]
    full = lambda a: pl.BlockSpec(a.shape, lambda i: (0,) * a.ndim)
    tile = pl.BlockSpec((tm, D_MODEL), lambda i: (i, 0))
    return pl.pallas_call(
        _ffn_body,
        grid=(n // tm,),
        in_specs=[tile, full(g1), full(wgu), full(wd), full(g2)],
        out_specs=tile,
        out_shape=jax.ShapeDtypeStruct((n, D_MODEL), F32),
        compiler_params=_params("parallel"),
        name="ffn",
    )(x, g1, wgu, wd, g2)


def _pad_rows(a, rows, at=0):
    out = jnp.zeros((rows, a.shape[1]), a.dtype)
    return lax.dynamic_update_slice(out, a, (at, 0))


def kernel(x, mix_pre_g, mix_post_g, ffn_pre_g, ffn_post_g, w_in, w_out, rwkv_mu, rwkv_w0, rwkv_w2,
           rwkv_a0, rwkv_a2, rwkv_g2, rwkv_v0, rwkv_v1, rwkv_v2, rwkv_k_k, rwkv_k_a, rwkv_r_k,
           rwkv_ln_w, rwkv_ln_b, fox_q_g, fox_k_g, fox_f_b, pool_w, pool_scale, ffn_w_gu, ffn_w_down):
    batch, seq_len, _ = x.shape
    n = batch * seq_len
    depth = w_in.shape[0]
    row = lambda a: a.reshape(1, -1).astype(F32)
    xf = x.reshape(n, D_MODEL)
    w_arr = _w_regroup(w_in)
    v_first = None
    for l in range(depth):
        mu = jnp.pad(rwkv_mu[l], (0, RWKV_PAD - RWKV_IN)).reshape(1, RWKV_PAD)
        v0 = rwkv_v0[l - 1] if l > 0 else jnp.zeros((MIX_W,), F32)
        vecs = jnp.stack([rwkv_w0[l], rwkv_a0[l], rwkv_k_k[l], rwkv_k_a[l], rwkv_r_k[l].reshape(-1), v0,
                          jnp.zeros((MIX_W,), F32), jnp.zeros((MIX_W,), F32)])
        w2p = _pad_rows(rwkv_w2[l], 128, 0).astype(BF16)
        a2p = _pad_rows(rwkv_a2[l], 128, 64).astype(BF16)
        g2p = _pad_rows(rwkv_g2[l], 256, 0).astype(BF16)
        vres = None
        if l > 0:
            v1p = jnp.pad(rwkv_v1[l - 1], ((0, 0), (0, 128 - 32))).astype(BF16)
            v2p = _pad_rows(rwkv_v2[l - 1], 128, 0).astype(BF16)
            vres = (v_first, v1p, v2p)
        tile6 = lambda a: jnp.tile(a, N_HEADS).reshape(1, MIX_W)
        fb = jnp.pad(fox_f_b[l], (0, FF_PAD - N_HEADS)).reshape(1, FF_PAD)
        qg, kg = tile6(fox_q_g[l]), tile6(fox_k_g[l])
        (r, k, v, lw, kk, b, g, bonus, q_a, k_a, v_t, cend, fg, pool_in) = _proj(
            seq_len, xf, row(mix_pre_g[l]), w_arr[l], mu, vecs, w2p, a2p, g2p, qg, kg, fb, vres, FOX_BLOCK)
        if l == 0:
            v_first = v
        y_scan = _rwkv_scan(seq_len, r, lw, k, v, kk, b)
        attn = _fox_attn(seq_len, q_a, k_a, v_t, cend, qg, kg, FOX_BLOCK, FOX_Q_BLOCKS)

        ln = jnp.stack([rwkv_ln_w[l], rwkv_ln_b[l]] + [jnp.zeros((MIX_W,), F32)] * 6)
        pw = jax.scipy.linalg.block_diag(*[pool_w[l, gi] for gi in range(4)]).astype(BF16)
        xf = _mix_out(seq_len, xf, y_scan, g, bonus, attn, fg, pool_in, ln, pw,
                      row(pool_scale[l]), w_out[l].astype(BF16), row(mix_post_g[l]))
        xf = _ffn(xf, row(ffn_pre_g[l]), ffn_w_gu[l].astype(BF16), ffn_w_down[l].astype(BF16),
                  row(ffn_post_g[l]))
    return xf.reshape(batch, seq_len, D_MODEL)
```

```python
import functools

import jax
import jax.numpy as jnp
from jax import lax
from jax.experimental import pallas as pl
from jax.experimental.pallas import tpu as pltpu

F32 = jnp.float32
BF16 = jnp.bfloat16

D_MODEL = 1024
HEAD_DIM = 64
PAIR = 2 * HEAD_DIM
N_HEADS = 6
N_PAIRS = N_HEADS // 2
MIX_W = N_HEADS * HEAD_DIM
POOL_W = 256
POOL_HALO = 16
FFN_HIDDEN = 2816
FFN_CHUNK = 256
RMS_EPS = 1e-6
GN_EPS = HEAD_DIM * 1e-5

RWKV_IN = 1440
RWKV_PAD = 1536
FOX_IN = 4 * MIX_W
FF_PAD = 128
IN_SPLITS = ((0, 1536), (1536, 3072), (3072, 3200), (3200, 3456))
IN_TOTAL = 3456

LOG2E = 1.4426950408889634
FOX_BLOCK = 256
FOX_Q_BLOCKS = 2
FOX_WIDE = 4
UNDERFLOW_CUT = 110.0
BF16_MARGIN = 1.02
SCAN_CHUNK = 64
VMEM_LIMIT = 56 * 1024 * 1024


def _params(*sem):
    return pltpu.CompilerParams(dimension_semantics=sem, vmem_limit_bytes=VMEM_LIMIT)


def _dot(a, b):
    return jnp.dot(a.astype(BF16), b.astype(BF16), preferred_element_type=F32)


def _dot_nt(a, b):
    return lax.dot_general(a.astype(BF16), b.astype(BF16), (((1,), (1,)), ((), ())),
                           preferred_element_type=F32)


def _split3(x):
    hi = x.astype(BF16)
    r1 = x - hi.astype(F32)
    mid = r1.astype(BF16)
    lo = (r1 - mid.astype(F32)).astype(BF16)
    return hi, mid, lo


def _head_sums(x):
    left = lax.broadcasted_iota(jnp.int32, (1, PAIR), 1) < HEAD_DIM
    outs = []
    for p in range(x.shape[1] // PAIR):
        xp = x[:, p * PAIR:(p + 1) * PAIR]
        lsum = jnp.sum(jnp.where(left, xp, 0.0), axis=1, keepdims=True)
        rsum = jnp.sum(jnp.where(left, 0.0, xp), axis=1, keepdims=True)
        outs.append(jnp.where(left, lsum, rsum))
    return jnp.concatenate(outs, axis=1)


def _dot3r(w, x):
    hi, mid, lo = _split3(x)
    d = lambda t: jnp.dot(w, t, preferred_element_type=F32)
    return d(hi) + d(mid) + d(lo)


def _rms(x, g):
    ms = jnp.mean(x * x, axis=-1, keepdims=True)
    return x * lax.rsqrt(ms + RMS_EPS) * g


def _softplus(z):
    return jnp.maximum(z, 0.0) + jnp.log1p(jnp.exp(-jnp.abs(z)))


def _sigmoid(z):
    return 1.0 / (1.0 + jnp.exp(-z))


def _tri(n, inclusive, dtype):
    r = lax.broadcasted_iota(jnp.int32, (n, n), 0)
    c = lax.broadcasted_iota(jnp.int32, (n, n), 1)
    keep = (r >= c) if inclusive else (r > c)
    return jnp.where(keep, 1.0, 0.0).astype(dtype)


def _w_regroup_body(w_ref, o_ref):
    w = w_ref[0]
    zeros = lambda c: jnp.zeros((w.shape[0], c), F32)
    fox0, ff0, pool0 = RWKV_IN, RWKV_IN + FOX_IN, RWKV_IN + FOX_IN + N_HEADS
    o_ref[0] = jnp.concatenate(
        [w[:, :fox0], zeros(RWKV_PAD - RWKV_IN), w[:, fox0:ff0], w[:, ff0:pool0],
         zeros(FF_PAD - N_HEADS), w[:, pool0:]], axis=1).astype(BF16)


def _w_regroup(w_in, rows=128):
    depth, d, cols = w_in.shape
    return pl.pallas_call(
        _w_regroup_body,
        grid=(depth, d // rows),
        in_specs=[pl.BlockSpec((1, rows, cols), lambda l, i: (l, i, 0))],
        out_specs=pl.BlockSpec((1, rows, IN_TOTAL), lambda l, i: (l, i, 0)),
        out_shape=jax.ShapeDtypeStruct((depth, d, IN_TOTAL), BF16),
        compiler_params=_params("parallel", "parallel"),
        name="w_regroup",
    )(w_in)


def _rwkv_features(f, halo, mu, vecs, w2, a2, g2, vres):
    tm = f.shape[0]
    row = lax.broadcasted_iota(jnp.int32, (tm, 1), 0)
    prev = jnp.where(row == 0, halo, pltpu.roll(f, 1, axis=0))
    f = f + (prev - f) * mu
    r, k, v = f[:, 0:384], f[:, 384:768], f[:, 768:1152]
    wa = f[:, 1152:1280]
    gd = f[:, 1280:1536]
    w0, a0, k_k, k_a, r_k, v0 = (vecs[j:j + 1, :] for j in range(6))

    w_log = -_softplus(-(w0 + _dot(jnp.tanh(wa), w2))) - 0.5
    lw = -jnp.exp(w_log)
    a = _sigmoid(a0 + _dot(wa, a2))
    g = _dot(_sigmoid(gd), g2)
    if vres is not None:
        v_first, v1, v2 = vres
        v = v + (v_first - v) * _sigmoid(v0 + _dot(_dot(v, v1), v2))
    kk = k * k_k
    kk = kk / jnp.maximum(jnp.sqrt(_head_sums(kk * kk)), 1e-12)
    k = k * (1.0 + (a - 1.0) * k_a)
    return r, k, v, lw, kk, kk * a, g, _head_sums(r * k * r_k) * v


def _scan_body(chunks_per_step, r_ref, lw_ref, k_ref, v_ref, kk_ref, b_ref, y_ref, h_ref):
    c_len = SCAN_CHUNK
    n_batch = r_ref.shape[0]

    @pl.when(pl.program_id(0) == 0)
    def _():
        h_ref[...] = jnp.zeros_like(h_ref)

    lane = lax.broadcasted_iota(jnp.int32, (1, PAIR), 1)
    left = lane < HEAD_DIM
    ri = lax.broadcasted_iota(jnp.int32, (PAIR, PAIR), 0)
    ci = lax.broadcasted_iota(jnp.int32, (PAIR, PAIR), 1)
    strict, incl, diag = ri > ci, ri >= ci, ri == ci
    tri = _tri(c_len, True, BF16)
    eye = jnp.where(diag, 1.0, 0.0)

    def stack(x):
        return jnp.concatenate([jnp.where(left, x, 0.0), jnp.where(left, 0.0, x)], axis=0)

    seqs = [(bi, p) for bi in range(n_batch) for p in range(N_PAIRS)]
    items = [(bi, p, c) for (bi, p) in seqs for c in range(chunks_per_step)]
    window = lambda bi, p, c: (bi, slice(c * c_len, (c + 1) * c_len), slice(p * PAIR, (p + 1) * PAIR))
    load = lambda ref: [ref[window(*it)] for it in items]
    each = lambda f, *cols: [f(*xs) for xs in zip(*cols)]

    lw = load(lw_ref)
    cum = each(lambda x: _dot3r(tri, x), lw)
    cum_end = each(lambda x: x[c_len - 1:c_len, :], cum)
    e_neg = each(lambda x: jnp.exp(-x), cum)
    e_tail = each(lambda x, xe: jnp.exp(xe - x), cum, cum_end)
    kk, b, k = load(kk_ref), load(b_ref), load(k_ref)
    l_a = each(lambda x, c_, w_: stack(-x * jnp.exp(c_ - w_)), kk, cum, lw)
    l_r = each(lambda x, c_: stack(x * jnp.exp(c_)), load(r_ref), cum)
    r_b = each(lambda x, e: stack(x * e), b, e_neg)
    r_k = each(lambda x, e: stack(x * e), k, e_neg)
    rb_t = each(lambda x, e: stack(x * e).T, b, e_tail)
    rk_t = each(lambda x, e: stack(x * e).T, k, e_tail)
    v_s = each(stack, load(v_ref))

    a_ab = each(lambda x, y: jnp.where(strict, _dot_nt(x, y), 0.0), l_a, r_b)
    a_ak = each(lambda x, y: jnp.where(strict, _dot_nt(x, y), 0.0), l_a, r_k)
    a_rb = each(lambda x, y: jnp.where(incl, _dot_nt(x, y), 0.0), l_r, r_b)
    a_rk = each(lambda x, y: jnp.where(incl, _dot_nt(x, y), 0.0), l_r, r_k)

    x = a_ab
    t_inv = each(lambda a: eye + a, x)
    for _ in range(5):
        x = each(lambda a: _dot(a, a), x)
        t_inv = each(lambda t, a: t + _dot(t, a), t_inv, x)

    t_a = each(_dot, t_inv, l_a)
    w_ = each(lambda t, a, v: _dot(t, _dot(a, v)), t_inv, a_ak, v_s)
    r_q = each(lambda l, a, t: l + _dot(a, t), l_r, a_rb, t_a)
    y_0 = each(lambda a, w, a2, v: _dot(a, w) + _dot(a2, v), a_rb, w_, a_rk, v_s)
    m_ = each(lambda ce, bt, t: jnp.where(diag, jnp.exp(ce), 0.0) + _dot(bt, t), cum_end, rb_t, t_a)
    n_ = each(lambda bt, w, kt, v: _dot(bt, w) + _dot(kt, v), rb_t, w_, rk_t, v_s)

    for s, (bi, p) in enumerate(seqs):
        h = h_ref[s]
        for c in range(chunks_per_step):
            idx = s * chunks_per_step + c
            y_st = _dot(r_q[idx], h) + y_0[idx]
            y_ref[window(bi, p, c)] = y_st[:c_len] + y_st[c_len:]
            h = _dot(m_[idx], h) + n_[idx]
        h_ref[s] = h


def _rwkv_scan(seq_len, r, lw, k, v, kk, b, chunks_per_step=2):
    n = r.shape[0]
    n_batch = n // seq_len
    rows = chunks_per_step * SCAN_CHUNK
    spec = pl.BlockSpec((n_batch, rows, MIX_W), lambda j: (0, j, 0))
    args = [a.reshape(n_batch, seq_len, MIX_W) for a in (r, lw, k, v, kk, b)]
    y = pl.pallas_call(
        functools.partial(_scan_body, chunks_per_step),
        grid=(seq_len // rows,),
        in_specs=[spec] * 6,
        out_specs=spec,
        out_shape=jax.ShapeDtypeStruct((n_batch, seq_len, MIX_W), F32),
        scratch_shapes=[pltpu.VMEM((n_batch * N_PAIRS, PAIR, PAIR), F32)],
        compiler_params=_params("arbitrary"),
        name="rwkv_scan",
    )(*args)
    return y.reshape(n, MIX_W)


def _fox_features(q, k, v, ff, qg, kg, fb):
    tm = q.shape[0]

    def qk_norm(x, g):
        ms = _head_sums(x * x) * (1.0 / HEAD_DIM)
        return x * lax.rsqrt(ms + RMS_EPS) * g

    lane = lax.broadcasted_iota(jnp.int32, (1, PAIR), 1)
    left = lane < HEAD_DIM

    def pad_heads(x, aug):
        cols = []
        for h in range(N_HEADS):
            src = x[:, (h // 2) * PAIR:(h // 2 + 1) * PAIR]
            if h % 2:
                src = pltpu.roll(src, HEAD_DIM, axis=1)
            cols.append(jnp.where(left, src, aug(h)))
        return jnp.concatenate(cols, axis=1).astype(BF16)

    log_f = -_softplus(-(ff + fb))
    local = _dot3r(_tri(tm, True, BF16), log_f)
    bias = local * (-LOG2E)

    def k_aug(h):
        col = jnp.sum(jnp.where(lane == h, bias, 0.0), axis=1, keepdims=True)
        hi, mid, lo = (p.astype(F32) for p in _split3(col))
        return jnp.where(lane == HEAD_DIM, hi,
                         jnp.where(lane == HEAD_DIM + 1, mid, jnp.where(lane == HEAD_DIM + 2, lo, 0.0)))

    q_aug = jnp.where((lane >= HEAD_DIM) & (lane < HEAD_DIM + 3), 1.0, 0.0)
    q_a = pad_heads(qk_norm(q, qg) * (LOG2E * HEAD_DIM ** -0.5), lambda h: q_aug)
    k_a = pad_heads(qk_norm(k, kg), k_aug)

    v_t = v.T
    ones_row = jnp.where(lax.broadcasted_iota(jnp.int32, (HEAD_DIM, tm), 0) == 0, 1.0, 0.0)
    rows = []
    for h in range(N_HEADS):
        rows += [v_t[h * HEAD_DIM:(h + 1) * HEAD_DIM, :], ones_row]
    return q_a, k_a, jnp.concatenate(rows, axis=0).astype(BF16), local


def _proj_body(seq_len, has_vres, *refs):
    n_in = 14 if has_vres else 11
    x_ref, g_ref, w_ref, mu_ref, vec_ref, w2_ref, a2_ref, g2_ref, qg_ref, kg_ref, fb_ref = refs[:11]
    vres_refs = refs[11:14] if has_vres else None
    (r_o, k_o, v_o, lw_o, kk_o, b_o, g_o, bonus_o,
     qa_o, ka_o, vt_o, cend_o, fg_o, pool_o) = refs[n_in:n_in + 14]
    prev_ref, carry_ref = refs[n_in + 14:]
    tm = x_ref.shape[0]
    blocks = seq_len // tm
    j = lax.rem(pl.program_id(0), blocks)

    @pl.when(j == 0)
    def _():
        prev_ref[...] = jnp.zeros_like(prev_ref)
        carry_ref[...] = jnp.zeros_like(carry_ref)
        cend_o[...] = jnp.zeros_like(cend_o)

    h = _rms(x_ref[...], g_ref[...]).astype(BF16)
    feats, fox, ff, pool = (jnp.dot(h, w_ref[:, c0:c1], preferred_element_type=F32) for c0, c1 in IN_SPLITS)
    pool_o[...] = pool
    fg_o[...] = fox[:, 3 * MIX_W:]

    halo = prev_ref[7:8, :]
    prev_ref[...] = feats[tm - 8:, :]
    vres = None if vres_refs is None else tuple(ref[...] for ref in vres_refs)
    outs = _rwkv_features(feats, halo, mu_ref[...], vec_ref[...], w2_ref[...], a2_ref[...], g2_ref[...], vres)
    for o_ref, val in zip((r_o, k_o, v_o, lw_o, kk_o, b_o, g_o, bonus_o), outs):
        o_ref[...] = val

    q_a, k_a, v_t, local = _fox_features(fox[:, :MIX_W], fox[:, MIX_W:2 * MIX_W], fox[:, 2 * MIX_W:3 * MIX_W],
                                         ff, qg_ref[...], kg_ref[...], fb_ref[...])
    qa_o[...] = q_a
    ka_o[...] = k_a
    vt_o[0] = v_t
    cum = local + carry_ref[...]
    carry_ref[...] = cum[tm - 1:tm, :]
    blk = lax.broadcasted_iota(jnp.int32, (1, blocks), 1)
    cend_o[0] = jnp.where(blk == j, cum.T[0:8, tm - 1:tm], cend_o[0])


def _proj(seq_len, x, g, w, layer, mu, vecs, w2p, a2p, g2p, qg, kg, fb, vres, tm):
    n = x.shape[0]
    blocks = seq_len // tm
    full = lambda a: pl.BlockSpec(a.shape, lambda i: (0,) * a.ndim)
    tile = lambda c: pl.BlockSpec((tm, c), lambda i: (i, 0))
    w_spec = pl.BlockSpec((None,) + w.shape[1:], lambda i: (layer, 0, 0))
    in_specs = ([tile(D_MODEL), full(g), w_spec]
                + [full(a) for a in (mu, vecs, w2p, a2p, g2p, qg, kg, fb)])
    args = [x, g, w, mu, vecs, w2p, a2p, g2p, qg, kg, fb]
    if vres is not None:
        v_first, v1p, v2p = vres
        in_specs += [tile(MIX_W), full(v1p), full(v2p)]
        args += [v_first, v1p, v2p]
    padded = N_HEADS * PAIR
    return pl.pallas_call(
        functools.partial(_proj_body, seq_len, vres is not None),
        grid=(n // tm,),
        in_specs=in_specs,
        out_specs=[tile(MIX_W)] * 8 + [tile(padded), tile(padded),
                                       pl.BlockSpec((1, padded, tm), lambda i: (i, 0, 0)),
                                       pl.BlockSpec((1, 8, blocks), lambda i: (i // blocks, 0, 0)),
                                       tile(MIX_W), tile(POOL_W)],
        out_shape=[jax.ShapeDtypeStruct((n, MIX_W), F32)] * 8
        + [jax.ShapeDtypeStruct((n, padded), BF16)] * 2
        + [jax.ShapeDtypeStruct((n // tm, padded, tm), BF16),
           jax.ShapeDtypeStruct((n // seq_len, 8, blocks), F32),
           jax.ShapeDtypeStruct((n, MIX_W), F32), jax.ShapeDtypeStruct((n, POOL_W), F32)],
        scratch_shapes=[pltpu.VMEM((8, RWKV_PAD), F32), pltpu.VMEM((1, FF_PAD), F32)],
        compiler_params=_params("arbitrary"),
        name="proj",
    )(*args)


def _fox_attn_body(q_ref, k_ref, vt_ref, cend_ref, qg_ref, kg_ref, o_ref):
    h = pl.program_id(1)
    tq = q_ref.shape[0]
    tk = vt_ref.shape[2]
    q_blocks = tq // tk
    diag = pl.program_id(2) * q_blocks
    nk = cend_ref.shape[2]
    q = q_ref[...]
    sub = lax.broadcasted_iota(jnp.int32, (8, 1), 0)
    blk = lax.broadcasted_iota(jnp.int32, (1, nk), 1)
    cend = jnp.sum(jnp.where(sub == h, cend_ref[0], 0.0), axis=0, keepdims=True)
    end_of = lambda b: jnp.sum(jnp.where(blk == b, cend, 0.0), axis=1, keepdims=True)
    c_ref = end_of(diag - 1)

    amax = lambda r: jnp.max(jnp.abs(r[...]), axis=1, keepdims=True)
    qk_bound = 8.0 * BF16_MARGIN * amax(qg_ref) * amax(kg_ref)
    needed = (blk < diag) & (c_ref - cend > -(2.0 * qk_bound + UNDERFLOW_CUT))
    first = diag - jnp.sum(jnp.where(needed, 1, 0))

    def block(j, width, carry, masked_from=None):
        m, acc = carry
        start = pl.multiple_of(j * tk, tk)
        z = lax.dot_general(k_ref[pl.ds(start, width * tk), :], q, (((1,), (1,)), ((), ())),
                            preferred_element_type=F32)
        for u in range(width):
            z_u = z[u * tk:(u + 1) * tk, :]
            if masked_from is not None and u >= masked_from:
                key = lax.broadcasted_iota(jnp.int32, (tk, tq), 0) + (u - masked_from) * tk
                qry = lax.broadcasted_iota(jnp.int32, (tk, tq), 1)
                z_u = jnp.where(key <= qry, z_u, -1e30)
            s_u = (end_of(j + u - 1) - c_ref) * LOG2E
            m_new = jnp.maximum(m, jnp.max(z_u, axis=0, keepdims=True) - s_u)
            p_u = jnp.exp2(z_u - (m_new + s_u)).astype(BF16)
            acc = acc * jnp.exp2(m - m_new) + jnp.dot(vt_ref[j + u], p_u, preferred_element_type=F32)
            m = m_new
        return m, acc

    def finish(carry):
        acc = carry[1]
        value_row = lax.broadcasted_iota(jnp.int32, (PAIR, 1), 0) < HEAD_DIM
        out_t = jnp.where(value_row, acc * (1.0 / acc[HEAD_DIM:HEAD_DIM + 1, :]), 0.0)
        o_ref[...] = out_t.T

    init = (jnp.full((1, tq), -1e30, F32), jnp.zeros((PAIR, tq), F32))
    lead = FOX_WIDE - q_blocks

    @pl.when(diag < lead)
    def _():
        carry = block(diag, q_blocks, init, 0)
        finish(lax.fori_loop(0, diag, lambda j, c: block(j, 1, c), carry))

    @pl.when(diag >= lead)
    def _():
        hi = diag - lead
        carry = block(hi, FOX_WIDE, init, lead)
        wide = jnp.minimum((hi - first + FOX_WIDE - 1) // FOX_WIDE, hi // FOX_WIDE)
        wide = jnp.maximum(wide, 0)
        lo = hi - wide * FOX_WIDE
        carry = lax.fori_loop(0, wide, lambda g, c: block(lo + g * FOX_WIDE, FOX_WIDE, c), carry)
        finish(lax.fori_loop(first, lo, lambda j, c: block(j, 1, c), carry))


def _fox_attn(seq_len, q, k, vt, cend, qg, kg, tk, q_blocks):
    n = q.shape[0]
    tq = tk * q_blocks
    nq, nk = seq_len // tq, seq_len // tk
    full = lambda a: pl.BlockSpec(a.shape, lambda bi, h, i: (0,) * a.ndim)
    return pl.pallas_call(
        _fox_attn_body,
        grid=(n // seq_len, N_HEADS, nq),
        in_specs=[pl.BlockSpec((tq, PAIR), lambda bi, h, i: (bi * nq + i, h)),
                  pl.BlockSpec((seq_len, PAIR), lambda bi, h, i: (bi, h)),
                  pl.BlockSpec((nk, PAIR, tk), lambda bi, h, i: (bi, h, 0)),
                  pl.BlockSpec((1, 8, nk), lambda bi, h, i: (bi, 0, 0)),
                  full(qg), full(kg)],
        out_specs=pl.BlockSpec((tq, PAIR), lambda bi, h, i: (bi * nq + i, h)),
        out_shape=jax.ShapeDtypeStruct((n, N_HEADS * PAIR), F32),
        compiler_params=_params("parallel", "parallel", "arbitrary"),
        name="fox_attn",
    )(q, k, vt, cend, qg, kg)


def _mix_out_body(seq_len, x_ref, ys_ref, g_ref, bonus_ref, at_ref, fg_ref, pool_ref, halo_ref,
                  ln_ref, pw_ref, ps_ref, wo_ref, pg_ref, o_ref):
    tm = x_ref.shape[0]
    row0 = lax.rem(pl.program_id(0) * tm, seq_len)

    y = ys_ref[...]
    d = y - _head_sums(y) * (1.0 / HEAD_DIM)
    var = _head_sums(d * d) * (1.0 / HEAD_DIM)
    y_rwkv = (d * lax.rsqrt(var + GN_EPS) * ln_ref[0:1, :] + ln_ref[1:2, :] + bonus_ref[...]) * g_ref[...]

    left = lax.broadcasted_iota(jnp.int32, (1, PAIR), 1) < HEAD_DIM
    at = at_ref[...]
    packed = [jnp.where(left, at[:, 2 * p * PAIR:(2 * p + 1) * PAIR],
                        pltpu.roll(at[:, (2 * p + 1) * PAIR:(2 * p + 2) * PAIR], HEAD_DIM, axis=1))
              for p in range(N_PAIRS)]
    y_fox = jnp.concatenate(packed, axis=1) * _sigmoid(fg_ref[...])

    pin = pool_ref[...]
    halo = jnp.where(row0 == 0, 0.0, halo_ref[...])
    s = jnp.concatenate([halo, pin], axis=0)
    sums = []
    for shift in (1, 2, 4, 8):
        s = s + pltpu.roll(s, shift, axis=0)
        sums.append(s[POOL_HALO:, :])
    grp = lax.broadcasted_iota(jnp.int32, (1, POOL_W), 1) // HEAD_DIM
    win_sum = jnp.where(grp == 0, sums[0], jnp.where(grp == 1, sums[1], jnp.where(grp == 2, sums[2], sums[3])))
    win = jnp.where(grp == 0, 2.0, jnp.where(grp == 1, 4.0, jnp.where(grp == 2, 8.0, 16.0)))
    pos = (row0 + lax.broadcasted_iota(jnp.int32, (tm, 1), 0) + 1).astype(F32)
    u = win_sum / jnp.minimum(pos, win) - pin
    y_pool = _dot(u, pw_ref[...]) * ps_ref[...]

    mixed = (_dot(y_rwkv, wo_ref[0:384, :]) + _dot(y_fox, wo_ref[384:768, :])
             + _dot(y_pool, wo_ref[768:1024, :]))
    o_ref[...] = x_ref[...] + _rms(mixed, pg_ref[...])


def _mix_out(seq_len, x, y_scan, g, bonus, attn, fg, pool_in, ln, pw, ps, wo, pg, tm=256):
    n = x.shape[0]
    full = lambda a: pl.BlockSpec(a.shape, lambda i: (0,) * a.ndim)
    tile = pl.BlockSpec((tm, MIX_W), lambda i: (i, 0))
    return pl.pallas_call(
        functools.partial(_mix_out_body, seq_len),
        grid=(n // tm,),
        in_specs=[pl.BlockSpec((tm, D_MODEL), lambda i: (i, 0)), tile, tile, tile,
                  pl.BlockSpec((tm, N_HEADS * PAIR), lambda i: (i, 0)), tile,
                  pl.BlockSpec((tm, POOL_W), lambda i: (i, 0)),
                  pl.BlockSpec((POOL_HALO, POOL_W),
                               lambda i: (jnp.maximum(i * (tm // POOL_HALO) - 1, 0), 0)),
                  full(ln), full(pw), full(ps), full(wo), full(pg)],
        out_specs=pl.BlockSpec((tm, D_MODEL), lambda i: (i, 0)),
        out_shape=jax.ShapeDtypeStruct((n, D_MODEL), F32),
        compiler_params=_params("parallel"),
        name="mix_out",
    )(x, y_scan, g, bonus, attn, fg, pool_in, pool_in, ln, pw, ps, wo, pg)


def _ffn_body(x_ref, g1_ref, wgu_ref, wd_ref, g2_ref, o_ref):
    x = x_ref[...]
    h = _rms(x, g1_ref[...]).astype(BF16)
    acc = jnp.zeros(x.shape, F32)
    for c in range(FFN_HIDDEN // FFN_CHUNK):
        c0 = c * FFN_CHUNK
        gate = jnp.dot(h, wgu_ref[:, c0:c0 + FFN_CHUNK], preferred_element_type=F32)
        up = jnp.dot(h, wgu_ref[:, FFN_HIDDEN + c0:FFN_HIDDEN + c0 + FFN_CHUNK],
                     preferred_element_type=F32)
        act = (gate * _sigmoid(gate) * up).astype(BF16)
        acc = acc + jnp.dot(act, wd_ref[c0:c0 + FFN_CHUNK, :], preferred_element_type=F32)
    o_ref[...] = x + _rms(acc, g2_ref[...])


def _ffn(x, g1, wgu, wd, g2, tm=512):
    n = x.shape[0]
    full = lambda a: pl.BlockSpec(a.shape, lambda i: (0,) * a.ndim)
    tile = pl.BlockSpec((tm, D_MODEL), lambda i: (i, 0))
    return pl.pallas_call(
        _ffn_body,
        grid=(n // tm,),
        in_specs=[tile, full(g1), full(wgu), full(wd), full(g2)],
        out_specs=tile,
        out_shape=jax.ShapeDtypeStruct((n, D_MODEL), F32),
        compiler_params=_params("parallel"),
        name="ffn",
    )(x, g1, wgu, wd, g2)


def _pad_rows(a, rows, at=0):
    out = jnp.zeros((rows, a.shape[1]), a.dtype)
    return lax.dynamic_update_slice(out, a, (at, 0))


def kernel(x, mix_pre_g, mix_post_g, ffn_pre_g, ffn_post_g, w_in, w_out, rwkv_mu, rwkv_w0, rwkv_w2,
           rwkv_a0, rwkv_a2, rwkv_g2, rwkv_v0, rwkv_v1, rwkv_v2, rwkv_k_k, rwkv_k_a, rwkv_r_k,
           rwkv_ln_w, rwkv_ln_b, fox_q_g, fox_k_g, fox_f_b, pool_w, pool_scale, ffn_w_gu, ffn_w_down):
    batch, seq_len, _ = x.shape
    n = batch * seq_len
    depth = w_in.shape[0]
    row = lambda a: a.reshape(1, -1).astype(F32)
    xf = x.reshape(n, D_MODEL)
    w_arr = _w_regroup(w_in)
    v_first = None
    for l in range(depth):
        mu = jnp.pad(rwkv_mu[l], (0, RWKV_PAD - RWKV_IN)).reshape(1, RWKV_PAD)
        v0 = rwkv_v0[l - 1] if l > 0 else jnp.zeros((MIX_W,), F32)
        vecs = jnp.stack([rwkv_w0[l], rwkv_a0[l], rwkv_k_k[l], rwkv_k_a[l], rwkv_r_k[l].reshape(-1), v0,
                          jnp.zeros((MIX_W,), F32), jnp.zeros((MIX_W,), F32)])
        w2p = _pad_rows(rwkv_w2[l], 128, 0).astype(BF16)
        a2p = _pad_rows(rwkv_a2[l], 128, 64).astype(BF16)
        g2p = _pad_rows(rwkv_g2[l], 256, 0).astype(BF16)
        vres = None
        if l > 0:
            v1p = jnp.pad(rwkv_v1[l - 1], ((0, 0), (0, 128 - 32))).astype(BF16)
            v2p = _pad_rows(rwkv_v2[l - 1], 128, 0).astype(BF16)
            vres = (v_first, v1p, v2p)
        tile6 = lambda a: jnp.tile(a, N_HEADS).reshape(1, MIX_W)
        fb = jnp.pad(fox_f_b[l], (0, FF_PAD - N_HEADS)).reshape(1, FF_PAD)
        qg, kg = tile6(fox_q_g[l]), tile6(fox_k_g[l])
        (r, k, v, lw, kk, b, g, bonus, q_a, k_a, v_t, cend, fg, pool_in) = _proj(
            seq_len, xf, row(mix_pre_g[l]), w_arr, l, mu, vecs, w2p, a2p, g2p, qg, kg, fb, vres, FOX_BLOCK)
        if l == 0:
            v_first = v
        y_scan = _rwkv_scan(seq_len, r, lw, k, v, kk, b)
        attn = _fox_attn(seq_len, q_a, k_a, v_t, cend, qg, kg, FOX_BLOCK, FOX_Q_BLOCKS)

        ln = jnp.stack([rwkv_ln_w[l], rwkv_ln_b[l]] + [jnp.zeros((MIX_W,), F32)] * 6)
        pw = jax.scipy.linalg.block_diag(*[pool_w[l, gi] for gi in range(4)]).astype(BF16)
        xf = _mix_out(seq_len, xf, y_scan, g, bonus, attn, fg, pool_in, ln, pw,
                      row(pool_scale[l]), w_out[l].astype(BF16), row(mix_post_g[l]))
        xf = _ffn(xf, row(ffn_pre_g[l]), ffn_w_gu[l].astype(BF16), ffn_w_down[l].astype(BF16),
                  row(ffn_post_g[l]))
    return xf.reshape(batch, seq_len, D_MODEL)
```

```python
import functools

import jax
import jax.numpy as jnp
from jax import lax
from jax.experimental import pallas as pl
from jax.experimental.pallas import tpu as pltpu

F32 = jnp.float32
BF16 = jnp.bfloat16

D_MODEL = 1024
HEAD_DIM = 64
PAIR = 2 * HEAD_DIM
N_HEADS = 6
N_PAIRS = N_HEADS // 2
MIX_W = N_HEADS * HEAD_DIM
POOL_W = 256
POOL_HALO = 16
FFN_HIDDEN = 2816
FFN_CHUNK = 256
RMS_EPS = 1e-6
GN_EPS = HEAD_DIM * 1e-5

RWKV_IN = 1440
RWKV_PAD = 1536
FOX_IN = 4 * MIX_W
FF_PAD = 128
IN_SPLITS = ((0, 1536), (1536, 3072), (3072, 3200), (3200, 3456))
IN_TOTAL = 3456

LOG2E = 1.4426950408889634
FOX_BLOCK = 256
FOX_Q_BLOCKS = 2
FOX_WIDE = 4
UNDERFLOW_CUT = 110.0
BF16_MARGIN = 1.02
SCAN_CHUNK = 64
VMEM_LIMIT = 56 * 1024 * 1024


def _params(*sem):
    return pltpu.CompilerParams(dimension_semantics=sem, vmem_limit_bytes=VMEM_LIMIT)


def _dot(a, b):
    return jnp.dot(a.astype(BF16), b.astype(BF16), preferred_element_type=F32)


def _dot_nt(a, b):
    return lax.dot_general(a.astype(BF16), b.astype(BF16), (((1,), (1,)), ((), ())),
                           preferred_element_type=F32)


def _split3(x):
    hi = x.astype(BF16)
    r1 = x - hi.astype(F32)
    mid = r1.astype(BF16)
    lo = (r1 - mid.astype(F32)).astype(BF16)
    return hi, mid, lo


def _head_sums(x):
    left = lax.broadcasted_iota(jnp.int32, (1, PAIR), 1) < HEAD_DIM
    outs = []
    for p in range(x.shape[1] // PAIR):
        xp = x[:, p * PAIR:(p + 1) * PAIR]
        lsum = jnp.sum(jnp.where(left, xp, 0.0), axis=1, keepdims=True)
        rsum = jnp.sum(jnp.where(left, 0.0, xp), axis=1, keepdims=True)
        outs.append(jnp.where(left, lsum, rsum))
    return jnp.concatenate(outs, axis=1)


def _dot3r(w, x):
    hi, mid, lo = _split3(x)
    d = lambda t: jnp.dot(w, t, preferred_element_type=F32)
    return d(hi) + d(mid) + d(lo)


def _rms(x, g):
    ms = jnp.mean(x * x, axis=-1, keepdims=True)
    return x * lax.rsqrt(ms + RMS_EPS) * g


def _softplus(z):
    return jnp.maximum(z, 0.0) + jnp.log1p(jnp.exp(-jnp.abs(z)))


def _sigmoid(z):
    return 1.0 / (1.0 + jnp.exp(-z))


def _tri(n, inclusive, dtype):
    r = lax.broadcasted_iota(jnp.int32, (n, n), 0)
    c = lax.broadcasted_iota(jnp.int32, (n, n), 1)
    keep = (r >= c) if inclusive else (r > c)
    return jnp.where(keep, 1.0, 0.0).astype(dtype)


def _w_regroup_body(w_ref, o_ref):
    w = w_ref[0]
    zeros = lambda c: jnp.zeros((w.shape[0], c), F32)
    fox0, ff0, pool0 = RWKV_IN, RWKV_IN + FOX_IN, RWKV_IN + FOX_IN + N_HEADS
    o_ref[0] = jnp.concatenate(
        [w[:, :fox0], zeros(RWKV_PAD - RWKV_IN), w[:, fox0:ff0], w[:, ff0:pool0],
         zeros(FF_PAD - N_HEADS), w[:, pool0:]], axis=1).astype(BF16)


def _w_regroup(w_in, rows=128):
    depth, d, cols = w_in.shape
    return pl.pallas_call(
        _w_regroup_body,
        grid=(depth, d // rows),
        in_specs=[pl.BlockSpec((1, rows, cols), lambda l, i: (l, i, 0))],
        out_specs=pl.BlockSpec((1, rows, IN_TOTAL), lambda l, i: (l, i, 0)),
        out_shape=jax.ShapeDtypeStruct((depth, d, IN_TOTAL), BF16),
        compiler_params=_params("parallel", "parallel"),
        name="w_regroup",
    )(w_in)


def _rwkv_features(f, halo, mu, vecs, w2, a2, g2, vres):
    tm = f.shape[0]
    row = lax.broadcasted_iota(jnp.int32, (tm, 1), 0)
    prev = jnp.where(row == 0, halo, pltpu.roll(f, 1, axis=0))
    f = f + (prev - f) * mu
    r, k, v = f[:, 0:384], f[:, 384:768], f[:, 768:1152]
    wa = f[:, 1152:1280]
    gd = f[:, 1280:1536]
    w0, a0, k_k, k_a, r_k, v0 = (vecs[j:j + 1, :] for j in range(6))

    w_log = -_softplus(-(w0 + _dot(jnp.tanh(wa), w2))) - 0.5
    lw = -jnp.exp(w_log)
    a = _sigmoid(a0 + _dot(wa, a2))
    g = _dot(_sigmoid(gd), g2)
    if vres is not None:
        v_first, v1, v2 = vres
        v = v + (v_first - v) * _sigmoid(v0 + _dot(_dot(v, v1), v2))
    kk = k * k_k
    kk = kk / jnp.maximum(jnp.sqrt(_head_sums(kk * kk)), 1e-12)
    k = k * (1.0 + (a - 1.0) * k_a)
    return r, k, v, lw, kk, kk * a, g, _head_sums(r * k * r_k) * v


def _scan_body(chunks_per_step, r_ref, lw_ref, k_ref, v_ref, kk_ref, b_ref, y_ref, h_ref):
    c_len = SCAN_CHUNK
    n_batch = r_ref.shape[0]

    @pl.when(pl.program_id(0) == 0)
    def _():
        h_ref[...] = jnp.zeros_like(h_ref)

    lane = lax.broadcasted_iota(jnp.int32, (1, PAIR), 1)
    left = lane < HEAD_DIM
    ri = lax.broadcasted_iota(jnp.int32, (c_len, PAIR), 0)
    ci = lax.broadcasted_iota(jnp.int32, (c_len, PAIR), 1) & (HEAD_DIM - 1)
    strict, incl, diag = ri > ci, ri >= ci, ri == ci
    tri = _tri(c_len, True, BF16)
    eye = jnp.where(diag, 1.0, 0.0)

    def stack(x):
        return jnp.concatenate([jnp.where(left, x, 0.0), jnp.where(left, 0.0, x)], axis=0)

    def side_t(x):
        z = stack(x).T
        return z[:c_len] + z[c_len:]

    seqs = [(bi, p) for bi in range(n_batch) for p in range(N_PAIRS)]
    items = [(bi, p, c) for (bi, p) in seqs for c in range(chunks_per_step)]
    window = lambda bi, p, c: (bi, slice(c * c_len, (c + 1) * c_len), slice(p * PAIR, (p + 1) * PAIR))
    load = lambda ref: [ref[window(*it)] for it in items]
    each = lambda f, *cols: [f(*xs) for xs in zip(*cols)]

    lw = load(lw_ref)
    cum = each(lambda x: _dot3r(tri, x), lw)
    cum_end = each(lambda x: x[c_len - 1:c_len, :], cum)
    e_neg = each(lambda x: jnp.exp(-x), cum)
    e_tail = each(lambda x, xe: jnp.exp(xe - x), cum, cum_end)
    kk, b, k = load(kk_ref), load(b_ref), load(k_ref)
    l_a = each(lambda x, c_, w_: -x * jnp.exp(c_ - w_), kk, cum, lw)
    l_r = each(lambda x, c_: x * jnp.exp(c_), load(r_ref), cum)
    r_b = each(lambda x, e: stack(x * e), b, e_neg)
    r_k = each(lambda x, e: stack(x * e), k, e_neg)
    rb_t = each(lambda x, e: side_t(x * e), b, e_tail)
    rk_t = each(lambda x, e: side_t(x * e), k, e_tail)
    v_s = each(stack, load(v_ref))

    cat = lambda *xs: jnp.concatenate(xs, axis=0)
    part = lambda x, i: x[i * c_len:(i + 1) * c_len]
    l_ar = each(cat, l_a, l_r)
    a_b = each(_dot_nt, l_ar, r_b)
    a_k = each(_dot_nt, l_ar, r_k)
    a_ab = each(lambda a: jnp.where(strict, part(a, 0), 0.0), a_b)
    a_rb = each(lambda a: jnp.where(incl, part(a, 1), 0.0), a_b)
    a_ak = each(lambda a: jnp.where(strict, part(a, 0), 0.0), a_k)
    a_rk = each(lambda a: jnp.where(incl, part(a, 1), 0.0), a_k)

    t_inv = each(lambda a: eye + a, a_ab)
    x = each(lambda a: _dot(a, stack(a)), a_ab)
    for _ in range(4):
        xt = each(lambda x_, t: _dot(cat(x_, t), stack(x_)), x, t_inv)
        x = each(lambda p_: part(p_, 0), xt)
        t_inv = each(lambda t, p_: t + part(p_, 1), t_inv, xt)
    t_inv = each(lambda t, x_: t + _dot(t, stack(x_)), t_inv, x)

    t_a = each(lambda t, l: stack(_dot(t, stack(l))), t_inv, l_a)
    av = each(lambda a, a2, kt, v: _dot(cat(a, a2, kt), v), a_ak, a_rk, rk_t, v_s)
    w_ = each(lambda t, p_: stack(_dot(t, stack(part(p_, 0)))), t_inv, av)
    on_ta = each(lambda a, bt, t: _dot(cat(a, bt), t), a_rb, rb_t, t_a)
    on_w = each(lambda a, bt, w: _dot(cat(a, bt), w), a_rb, rb_t, w_)
    r_q = each(lambda l, p_: l + part(p_, 0), l_r, on_ta)
    m_ = each(lambda ce, p_: jnp.where(diag, jnp.exp(ce), 0.0) + part(p_, 1), cum_end, on_ta)
    y_0 = each(lambda p_, q_: part(p_, 0) + part(q_, 1), on_w, av)
    n_ = each(lambda p_, q_: part(p_, 1) + part(q_, 2), on_w, av)

    for s, (bi, p) in enumerate(seqs):
        h = h_ref[s]
        for c in range(chunks_per_step):
            idx = s * chunks_per_step + c
            on_h = _dot(cat(r_q[idx], m_[idx]), stack(h))
            y_ref[window(bi, p, c)] = part(on_h, 0) + y_0[idx]
            h = part(on_h, 1) + n_[idx]
        h_ref[s] = h


def _rwkv_scan(seq_len, r, lw, k, v, kk, b, chunks_per_step=2):
    n = r.shape[0]
    n_batch = n // seq_len
    rows = chunks_per_step * SCAN_CHUNK
    spec = pl.BlockSpec((n_batch, rows, MIX_W), lambda j: (0, j, 0))
    args = [a.reshape(n_batch, seq_len, MIX_W) for a in (r, lw, k, v, kk, b)]
    y = pl.pallas_call(
        functools.partial(_scan_body, chunks_per_step),
        grid=(seq_len // rows,),
        in_specs=[spec] * 6,
        out_specs=spec,
        out_shape=jax.ShapeDtypeStruct((n_batch, seq_len, MIX_W), F32),
        scratch_shapes=[pltpu.VMEM((n_batch * N_PAIRS, HEAD_DIM, PAIR), F32)],
        compiler_params=_params("arbitrary"),
        name="rwkv_scan",
    )(*args)
    return y.reshape(n, MIX_W)


def _fox_features(q, k, v, ff, qg, kg, fb):
    tm = q.shape[0]

    def qk_norm(x, g):
        ms = _head_sums(x * x) * (1.0 / HEAD_DIM)
        return x * lax.rsqrt(ms + RMS_EPS) * g

    lane = lax.broadcasted_iota(jnp.int32, (1, PAIR), 1)
    left = lane < HEAD_DIM

    def pad_heads(x, aug):
        cols = []
        for h in range(N_HEADS):
            src = x[:, (h // 2) * PAIR:(h // 2 + 1) * PAIR]
            if h % 2:
                src = pltpu.roll(src, HEAD_DIM, axis=1)
            cols.append(jnp.where(left, src, aug(h)))
        return jnp.concatenate(cols, axis=1).astype(BF16)

    log_f = -_softplus(-(ff + fb))
    local = _dot3r(_tri(tm, True, BF16), log_f)
    bias = local * (-LOG2E)

    def k_aug(h):
        col = jnp.sum(jnp.where(lane == h, bias, 0.0), axis=1, keepdims=True)
        hi, mid, lo = (p.astype(F32) for p in _split3(col))
        return jnp.where(lane == HEAD_DIM, hi,
                         jnp.where(lane == HEAD_DIM + 1, mid, jnp.where(lane == HEAD_DIM + 2, lo, 0.0)))

    q_aug = jnp.where((lane >= HEAD_DIM) & (lane < HEAD_DIM + 3), 1.0, 0.0)
    q_a = pad_heads(qk_norm(q, qg) * (LOG2E * HEAD_DIM ** -0.5), lambda h: q_aug)
    k_a = pad_heads(qk_norm(k, kg), k_aug)

    v_t = v.T
    ones_row = jnp.where(lax.broadcasted_iota(jnp.int32, (HEAD_DIM, tm), 0) == 0, 1.0, 0.0)
    rows = []
    for h in range(N_HEADS):
        rows += [v_t[h * HEAD_DIM:(h + 1) * HEAD_DIM, :], ones_row]
    return q_a, k_a, jnp.concatenate(rows, axis=0).astype(BF16), local


def _proj_body(seq_len, has_vres, *refs):
    n_in = 14 if has_vres else 11
    x_ref, g_ref, w_ref, mu_ref, vec_ref, w2_ref, a2_ref, g2_ref, qg_ref, kg_ref, fb_ref = refs[:11]
    vres_refs = refs[11:14] if has_vres else None
    (r_o, k_o, v_o, lw_o, kk_o, b_o, g_o, bonus_o,
     qa_o, ka_o, vt_o, cend_o, fg_o, pool_o) = refs[n_in:n_in + 14]
    prev_ref, carry_ref = refs[n_in + 14:]
    tm = x_ref.shape[0]
    blocks = seq_len // tm
    j = lax.rem(pl.program_id(0), blocks)

    @pl.when(j == 0)
    def _():
        prev_ref[...] = jnp.zeros_like(prev_ref)
        carry_ref[...] = jnp.zeros_like(carry_ref)
        cend_o[...] = jnp.zeros_like(cend_o)

    h = _rms(x_ref[...], g_ref[...]).astype(BF16)
    feats, fox, ff, pool = (jnp.dot(h, w_ref[:, c0:c1], preferred_element_type=F32) for c0, c1 in IN_SPLITS)
    pool_o[...] = pool
    fg_o[...] = fox[:, 3 * MIX_W:]

    halo = prev_ref[7:8, :]
    prev_ref[...] = feats[tm - 8:, :]
    vres = None if vres_refs is None else tuple(ref[...] for ref in vres_refs)
    outs = _rwkv_features(feats, halo, mu_ref[...], vec_ref[...], w2_ref[...], a2_ref[...], g2_ref[...], vres)
    for o_ref, val in zip((r_o, k_o, v_o, lw_o, kk_o, b_o, g_o, bonus_o), outs):
        o_ref[...] = val

    q_a, k_a, v_t, local = _fox_features(fox[:, :MIX_W], fox[:, MIX_W:2 * MIX_W], fox[:, 2 * MIX_W:3 * MIX_W],
                                         ff, qg_ref[...], kg_ref[...], fb_ref[...])
    qa_o[...] = q_a
    ka_o[...] = k_a
    vt_o[0] = v_t
    cum = local + carry_ref[...]
    carry_ref[...] = cum[tm - 1:tm, :]
    blk = lax.broadcasted_iota(jnp.int32, (1, blocks), 1)
    cend_o[0] = jnp.where(blk == j, cum.T[0:8, tm - 1:tm], cend_o[0])


def _proj(seq_len, x, g, w, mu, vecs, w2p, a2p, g2p, qg, kg, fb, vres, tm):
    n = x.shape[0]
    blocks = seq_len // tm
    full = lambda a: pl.BlockSpec(a.shape, lambda i: (0,) * a.ndim)
    tile = lambda c: pl.BlockSpec((tm, c), lambda i: (i, 0))
    in_specs = [tile(D_MODEL)] + [full(a) for a in (g, w, mu, vecs, w2p, a2p, g2p, qg, kg, fb)]
    args = [x, g, w, mu, vecs, w2p, a2p, g2p, qg, kg, fb]
    if vres is not None:
        v_first, v1p, v2p = vres
        in_specs += [tile(MIX_W), full(v1p), full(v2p)]
        args += [v_first, v1p, v2p]
    padded = N_HEADS * PAIR
    return pl.pallas_call(
        functools.partial(_proj_body, seq_len, vres is not None),
        grid=(n // tm,),
        in_specs=in_specs,
        out_specs=[tile(MIX_W)] * 8 + [tile(padded), tile(padded),
                                       pl.BlockSpec((1, padded, tm), lambda i: (i, 0, 0)),
                                       pl.BlockSpec((1, 8, blocks), lambda i: (i // blocks, 0, 0)),
                                       tile(MIX_W), tile(POOL_W)],
        out_shape=[jax.ShapeDtypeStruct((n, MIX_W), F32)] * 8
        + [jax.ShapeDtypeStruct((n, padded), BF16)] * 2
        + [jax.ShapeDtypeStruct((n // tm, padded, tm), BF16),
           jax.ShapeDtypeStruct((n // seq_len, 8, blocks), F32),
           jax.ShapeDtypeStruct((n, MIX_W), F32), jax.ShapeDtypeStruct((n, POOL_W), F32)],
        scratch_shapes=[pltpu.VMEM((8, RWKV_PAD), F32), pltpu.VMEM((1, FF_PAD), F32)],
        compiler_params=_params("arbitrary"),
        name="proj",
    )(*args)


def _fox_attn_body(q_ref, k_ref, vt_ref, cend_ref, qg_ref, kg_ref, o_ref):
    h = pl.program_id(1)
    tq = q_ref.shape[0]
    tk = vt_ref.shape[2]
    q_blocks = tq // tk
    diag = pl.program_id(2) * q_blocks
    nk = cend_ref.shape[2]
    q = q_ref[...]
    sub = lax.broadcasted_iota(jnp.int32, (8, 1), 0)
    blk = lax.broadcasted_iota(jnp.int32, (1, nk), 1)
    cend = jnp.sum(jnp.where(sub == h, cend_ref[0], 0.0), axis=0, keepdims=True)
    end_of = lambda b: jnp.sum(jnp.where(blk == b, cend, 0.0), axis=1, keepdims=True)
    c_ref = end_of(diag - 1)

    amax = lambda r: jnp.max(jnp.abs(r[...]), axis=1, keepdims=True)
    qk_bound = 8.0 * BF16_MARGIN * amax(qg_ref) * amax(kg_ref)
    needed = (blk < diag) & (c_ref - cend > -(2.0 * qk_bound + UNDERFLOW_CUT))
    first = diag - jnp.sum(jnp.where(needed, 1, 0))

    def block(j, width, carry, diagonal):
        m, acc = carry
        start = pl.multiple_of(j * tk, tk)
        z = lax.dot_general(k_ref[pl.ds(start, width * tk), :], q, (((1,), (1,)), ((), ())),
                            preferred_element_type=F32)
        if diagonal:
            key = lax.broadcasted_iota(jnp.int32, z.shape, 0)
            qry = lax.broadcasted_iota(jnp.int32, z.shape, 1)
            z = jnp.where(key <= qry, z, -1e30)
        for u in range(width):
            z_u = z[u * tk:(u + 1) * tk, :]
            s_u = (end_of(j + u - 1) - c_ref) * LOG2E
            m_new = jnp.maximum(m, jnp.max(z_u, axis=0, keepdims=True) - s_u)
            p_u = jnp.exp2(z_u - (m_new + s_u)).astype(BF16)
            acc = acc * jnp.exp2(m - m_new) + jnp.dot(vt_ref[j + u], p_u, preferred_element_type=F32)
            m = m_new
        return m, acc

    carry = (jnp.full((1, tq), -1e30, F32), jnp.zeros((PAIR, tq), F32))
    carry = block(diag, q_blocks, carry, True)
    i = diag
    wide = jnp.minimum((i - first + FOX_WIDE - 1) // FOX_WIDE, i // FOX_WIDE)
    lo = i - wide * FOX_WIDE
    carry = lax.fori_loop(0, wide, lambda g, c: block(lo + g * FOX_WIDE, FOX_WIDE, c, False), carry)
    carry = lax.fori_loop(first, lo, lambda j, c: block(j, 1, c, False), carry)
    acc = carry[1]
    value_row = lax.broadcasted_iota(jnp.int32, (PAIR, 1), 0) < HEAD_DIM
    out_t = jnp.where(value_row, acc * (1.0 / acc[HEAD_DIM:HEAD_DIM + 1, :]), 0.0)
    o_ref[...] = out_t.T


def _fox_attn(seq_len, q, k, vt, cend, qg, kg, tk, q_blocks):
    n = q.shape[0]
    tq = tk * q_blocks
    nq, nk = seq_len // tq, seq_len // tk
    full = lambda a: pl.BlockSpec(a.shape, lambda bi, h, i: (0,) * a.ndim)
    return pl.pallas_call(
        _fox_attn_body,
        grid=(n // seq_len, N_HEADS, nq),
        in_specs=[pl.BlockSpec((tq, PAIR), lambda bi, h, i: (bi * nq + i, h)),
                  pl.BlockSpec((seq_len, PAIR), lambda bi, h, i: (bi, h)),
                  pl.BlockSpec((nk, PAIR, tk), lambda bi, h, i: (bi, h, 0)),
                  pl.BlockSpec((1, 8, nk), lambda bi, h, i: (bi, 0, 0)),
                  full(qg), full(kg)],
        out_specs=pl.BlockSpec((tq, PAIR), lambda bi, h, i: (bi * nq + i, h)),
        out_shape=jax.ShapeDtypeStruct((n, N_HEADS * PAIR), F32),
        compiler_params=_params("parallel", "parallel", "arbitrary"),
        name="fox_attn",
    )(q, k, vt, cend, qg, kg)


def _mix_out_body(seq_len, x_ref, ys_ref, g_ref, bonus_ref, at_ref, fg_ref, pool_ref, halo_ref,
                  ln_ref, pw_ref, ps_ref, wo_ref, pg_ref, o_ref):
    tm = x_ref.shape[0]
    row0 = lax.rem(pl.program_id(0) * tm, seq_len)

    y = ys_ref[...]
    d = y - _head_sums(y) * (1.0 / HEAD_DIM)
    var = _head_sums(d * d) * (1.0 / HEAD_DIM)
    y_rwkv = (d * lax.rsqrt(var + GN_EPS) * ln_ref[0:1, :] + ln_ref[1:2, :] + bonus_ref[...]) * g_ref[...]

    left = lax.broadcasted_iota(jnp.int32, (1, PAIR), 1) < HEAD_DIM
    at = at_ref[...]
    packed = [jnp.where(left, at[:, 2 * p * PAIR:(2 * p + 1) * PAIR],
                        pltpu.roll(at[:, (2 * p + 1) * PAIR:(2 * p + 2) * PAIR], HEAD_DIM, axis=1))
              for p in range(N_PAIRS)]
    y_fox = jnp.concatenate(packed, axis=1) * _sigmoid(fg_ref[...])

    pin = pool_ref[...]
    halo = jnp.where(row0 == 0, 0.0, halo_ref[...])
    s = jnp.concatenate([halo, pin], axis=0)
    sums = []
    for shift in (1, 2, 4, 8):
        s = s + pltpu.roll(s, shift, axis=0)
        sums.append(s[POOL_HALO:, :])
    grp = lax.broadcasted_iota(jnp.int32, (1, POOL_W), 1) // HEAD_DIM
    win_sum = jnp.where(grp == 0, sums[0], jnp.where(grp == 1, sums[1], jnp.where(grp == 2, sums[2], sums[3])))
    win = jnp.where(grp == 0, 2.0, jnp.where(grp == 1, 4.0, jnp.where(grp == 2, 8.0, 16.0)))
    pos = (row0 + lax.broadcasted_iota(jnp.int32, (tm, 1), 0) + 1).astype(F32)
    u = win_sum / jnp.minimum(pos, win) - pin
    y_pool = _dot(u, pw_ref[...]) * ps_ref[...]

    mixed = (_dot(y_rwkv, wo_ref[0:384, :]) + _dot(y_fox, wo_ref[384:768, :])
             + _dot(y_pool, wo_ref[768:1024, :]))
    o_ref[...] = x_ref[...] + _rms(mixed, pg_ref[...])


def _mix_out(seq_len, x, y_scan, g, bonus, attn, fg, pool_in, ln, pw, ps, wo, pg, tm=256):
    n = x.shape[0]
    full = lambda a: pl.BlockSpec(a.shape, lambda i: (0,) * a.ndim)
    tile = pl.BlockSpec((tm, MIX_W), lambda i: (i, 0))
    return pl.pallas_call(
        functools.partial(_mix_out_body, seq_len),
        grid=(n // tm,),
        in_specs=[pl.BlockSpec((tm, D_MODEL), lambda i: (i, 0)), tile, tile, tile,
                  pl.BlockSpec((tm, N_HEADS * PAIR), lambda i: (i, 0)), tile,
                  pl.BlockSpec((tm, POOL_W), lambda i: (i, 0)),
                  pl.BlockSpec((POOL_HALO, POOL_W),
                               lambda i: (jnp.maximum(i * (tm // POOL_HALO) - 1, 0), 0)),
                  full(ln), full(pw), full(ps), full(wo), full(pg)],
        out_specs=pl.BlockSpec((tm, D_MODEL), lambda i: (i, 0)),
        out_shape=jax.ShapeDtypeStruct((n, D_MODEL), F32),
        compiler_params=_params("parallel"),
        name="mix_out",
    )(x, y_scan, g, bonus, attn, fg, pool_in, pool_in, ln, pw, ps, wo, pg)


def _ffn_body(x_ref, g1_ref, wgu_ref, wd_ref, g2_ref, o_ref):
    x = x_ref[...]
    h = _rms(x, g1_ref[...]).astype(BF16)
    acc = jnp.zeros(x.shape, F32)
    for c in range(FFN_HIDDEN // FFN_CHUNK):
        c0 = c * FFN_CHUNK
        gate = jnp.dot(h, wgu_ref[:, c0:c0 + FFN_CHUNK], preferred_element_type=F32)
        up = jnp.dot(h, wgu_ref[:, FFN_HIDDEN + c0:FFN_HIDDEN + c0 + FFN_CHUNK],
                     preferred_element_type=F32)
        act = (gate * _sigmoid(gate) * up).astype(BF16)
        acc = acc + jnp.dot(act, wd_ref[c0:c0 + FFN_CHUNK, :], preferred_element_type=F32)
    o_ref[...] = x + _rms(acc, g2_ref[...])


def _ffn(x, g1, wgu, wd, g2, tm=512):
    n = x.shape[0]
    full = lambda a: pl.BlockSpec(a.shape, lambda i: (0,) * a.ndim)
    tile = pl.BlockSpec((tm, D_MODEL), lambda i: (i, 0))
    return pl.pallas_call(
        _ffn_body,
        grid=(n // tm,),
        in_specs=[tile, full(g1), full(wgu), full(wd), full(g2)],
        out_specs=tile,
        out_shape=jax.ShapeDtypeStruct((n, D_MODEL), F32),
        compiler_params=_params("parallel"),
        name="ffn",
    )(x, g1, wgu, wd, g2)


def _pad_rows(a, rows, at=0):
    out = jnp.zeros((rows, a.shape[1]), a.dtype)
    return lax.dynamic_update_slice(out, a, (at, 0))


def kernel(x, mix_pre_g, mix_post_g, ffn_pre_g, ffn_post_g, w_in, w_out, rwkv_mu, rwkv_w0, rwkv_w2,
           rwkv_a0, rwkv_a2, rwkv_g2, rwkv_v0, rwkv_v1, rwkv_v2, rwkv_k_k, rwkv_k_a, rwkv_r_k,
           rwkv_ln_w, rwkv_ln_b, fox_q_g, fox_k_g, fox_f_b, pool_w, pool_scale, ffn_w_gu, ffn_w_down):
    batch, seq_len, _ = x.shape
    n = batch * seq_len
    depth = w_in.shape[0]
    row = lambda a: a.reshape(1, -1).astype(F32)
    xf = x.reshape(n, D_MODEL)
    w_arr = _w_regroup(w_in)
    v_first = None
    for l in range(depth):
        mu = jnp.pad(rwkv_mu[l], (0, RWKV_PAD - RWKV_IN)).reshape(1, RWKV_PAD)
        v0 = rwkv_v0[l - 1] if l > 0 else jnp.zeros((MIX_W,), F32)
        vecs = jnp.stack([rwkv_w0[l], rwkv_a0[l], rwkv_k_k[l], rwkv_k_a[l], rwkv_r_k[l].reshape(-1), v0,
                          jnp.zeros((MIX_W,), F32), jnp.zeros((MIX_W,), F32)])
        w2p = _pad_rows(rwkv_w2[l], 128, 0).astype(BF16)
        a2p = _pad_rows(rwkv_a2[l], 128, 64).astype(BF16)
        g2p = _pad_rows(rwkv_g2[l], 256, 0).astype(BF16)
        vres = None
        if l > 0:
            v1p = jnp.pad(rwkv_v1[l - 1], ((0, 0), (0, 128 - 32))).astype(BF16)
            v2p = _pad_rows(rwkv_v2[l - 1], 128, 0).astype(BF16)
            vres = (v_first, v1p, v2p)
        tile6 = lambda a: jnp.tile(a, N_HEADS).reshape(1, MIX_W)
        fb = jnp.pad(fox_f_b[l], (0, FF_PAD - N_HEADS)).reshape(1, FF_PAD)
        qg, kg = tile6(fox_q_g[l]), tile6(fox_k_g[l])
        (r, k, v, lw, kk, b, g, bonus, q_a, k_a, v_t, cend, fg, pool_in) = _proj(
            seq_len, xf, row(mix_pre_g[l]), w_arr[l], mu, vecs, w2p, a2p, g2p, qg, kg, fb, vres, FOX_BLOCK)
        if l == 0:
            v_first = v
        y_scan = _rwkv_scan(seq_len, r, lw, k, v, kk, b)
        attn = _fox_attn(seq_len, q_a, k_a, v_t, cend, qg, kg, FOX_BLOCK, FOX_Q_BLOCKS)

        ln = jnp.stack([rwkv_ln_w[l], rwkv_ln_b[l]] + [jnp.zeros((MIX_W,), F32)] * 6)
        pw = jax.scipy.linalg.block_diag(*[pool_w[l, gi] for gi in range(4)]).astype(BF16)
        xf = _mix_out(seq_len, xf, y_scan, g, bonus, attn, fg, pool_in, ln, pw,
                      row(pool_scale[l]), w_out[l].astype(BF16), row(mix_post_g[l]))
        xf = _ffn(xf, row(ffn_pre_g[l]), ffn_w_gu[l].astype(BF16), ffn_w_down[l].astype(BF16),
                  row(ffn_post_g[l]))
    return xf.reshape(batch, seq_len, D_MODEL)
```

```python
import functools

import jax
import jax.numpy as jnp
from jax import lax
from jax.experimental import pallas as pl
from jax.experimental.pallas import tpu as pltpu

F32 = jnp.float32
BF16 = jnp.bfloat16

D_MODEL = 1024
HEAD_DIM = 64
PAIR = 2 * HEAD_DIM
N_HEADS = 6
N_PAIRS = N_HEADS // 2
MIX_W = N_HEADS * HEAD_DIM
POOL_W = 256
POOL_HALO = 16
FFN_HIDDEN = 2816
FFN_CHUNK = 256
RMS_EPS = 1e-6
GN_EPS = HEAD_DIM * 1e-5

RWKV_IN = 1440
RWKV_PAD = 1536
FOX_IN = 4 * MIX_W
FF_PAD = 128
IN_SPLITS = ((0, 1536), (1536, 3072), (3072, 3200), (3200, 3456))
IN_TOTAL = 3456

LOG2E = 1.4426950408889634
FOX_BLOCK = 256
FOX_Q_BLOCKS = 2
FOX_WIDE = 8
UNDERFLOW_CUT = 110.0
BF16_MARGIN = 1.02
SCAN_CHUNK = 64
VMEM_LIMIT = 56 * 1024 * 1024


def _params(*sem):
    return pltpu.CompilerParams(dimension_semantics=sem, vmem_limit_bytes=VMEM_LIMIT)


def _dot(a, b):
    return jnp.dot(a.astype(BF16), b.astype(BF16), preferred_element_type=F32)


def _dot_nt(a, b):
    return lax.dot_general(a.astype(BF16), b.astype(BF16), (((1,), (1,)), ((), ())),
                           preferred_element_type=F32)


def _split3(x):
    hi = x.astype(BF16)
    r1 = x - hi.astype(F32)
    mid = r1.astype(BF16)
    lo = (r1 - mid.astype(F32)).astype(BF16)
    return hi, mid, lo


def _head_sums(x):
    left = lax.broadcasted_iota(jnp.int32, (1, PAIR), 1) < HEAD_DIM
    outs = []
    for p in range(x.shape[1] // PAIR):
        xp = x[:, p * PAIR:(p + 1) * PAIR]
        lsum = jnp.sum(jnp.where(left, xp, 0.0), axis=1, keepdims=True)
        rsum = jnp.sum(jnp.where(left, 0.0, xp), axis=1, keepdims=True)
        outs.append(jnp.where(left, lsum, rsum))
    return jnp.concatenate(outs, axis=1)


def _dot3r(w, x):
    hi, mid, lo = _split3(x)
    d = lambda t: jnp.dot(w, t, preferred_element_type=F32)
    return d(hi) + d(mid) + d(lo)


def _rms(x, g):
    ms = jnp.mean(x * x, axis=-1, keepdims=True)
    return x * lax.rsqrt(ms + RMS_EPS) * g


def _softplus(z):
    return jnp.maximum(z, 0.0) + jnp.log1p(jnp.exp(-jnp.abs(z)))


def _sigmoid(z):
    return 1.0 / (1.0 + jnp.exp(-z))


def _tri(n, inclusive, dtype):
    r = lax.broadcasted_iota(jnp.int32, (n, n), 0)
    c = lax.broadcasted_iota(jnp.int32, (n, n), 1)
    keep = (r >= c) if inclusive else (r > c)
    return jnp.where(keep, 1.0, 0.0).astype(dtype)


def _w_regroup_body(w_ref, o_ref):
    w = w_ref[0]
    zeros = lambda c: jnp.zeros((w.shape[0], c), F32)
    fox0, ff0, pool0 = RWKV_IN, RWKV_IN + FOX_IN, RWKV_IN + FOX_IN + N_HEADS
    o_ref[0] = jnp.concatenate(
        [w[:, :fox0], zeros(RWKV_PAD - RWKV_IN), w[:, fox0:ff0], w[:, ff0:pool0],
         zeros(FF_PAD - N_HEADS), w[:, pool0:]], axis=1).astype(BF16)


def _w_regroup(w_in, rows=128):
    depth, d, cols = w_in.shape
    return pl.pallas_call(
        _w_regroup_body,
        grid=(depth, d // rows),
        in_specs=[pl.BlockSpec((1, rows, cols), lambda l, i: (l, i, 0))],
        out_specs=pl.BlockSpec((1, rows, IN_TOTAL), lambda l, i: (l, i, 0)),
        out_shape=jax.ShapeDtypeStruct((depth, d, IN_TOTAL), BF16),
        compiler_params=_params("parallel", "parallel"),
        name="w_regroup",
    )(w_in)


def _rwkv_features(f, halo, mu, vecs, w2, a2, g2, vres):
    tm = f.shape[0]
    row = lax.broadcasted_iota(jnp.int32, (tm, 1), 0)
    prev = jnp.where(row == 0, halo, pltpu.roll(f, 1, axis=0))
    f = f + (prev - f) * mu
    r, k, v = f[:, 0:384], f[:, 384:768], f[:, 768:1152]
    wa = f[:, 1152:1280]
    gd = f[:, 1280:1536]
    w0, a0, k_k, k_a, r_k, v0 = (vecs[j:j + 1, :] for j in range(6))

    w_log = -_softplus(-(w0 + _dot(jnp.tanh(wa), w2))) - 0.5
    lw = -jnp.exp(w_log)
    a = _sigmoid(a0 + _dot(wa, a2))
    g = _dot(_sigmoid(gd), g2)
    if vres is not None:
        v_first, v1, v2 = vres
        v = v + (v_first - v) * _sigmoid(v0 + _dot(_dot(v, v1), v2))
    kk = k * k_k
    kk = kk / jnp.maximum(jnp.sqrt(_head_sums(kk * kk)), 1e-12)
    k = k * (1.0 + (a - 1.0) * k_a)
    return r, k, v, lw, kk, kk * a, g, _head_sums(r * k * r_k) * v


def _scan_body(chunks_per_step, r_ref, lw_ref, k_ref, v_ref, kk_ref, b_ref, y_ref, h_ref):
    c_len = SCAN_CHUNK
    n_batch = r_ref.shape[0]

    @pl.when(pl.program_id(0) == 0)
    def _():
        h_ref[...] = jnp.zeros_like(h_ref)

    lane = lax.broadcasted_iota(jnp.int32, (1, PAIR), 1)
    left = lane < HEAD_DIM
    ri = lax.broadcasted_iota(jnp.int32, (c_len, PAIR), 0)
    ci = lax.broadcasted_iota(jnp.int32, (c_len, PAIR), 1) & (HEAD_DIM - 1)
    strict, incl, diag = ri > ci, ri >= ci, ri == ci
    tri = _tri(c_len, True, BF16)
    eye = jnp.where(diag, 1.0, 0.0)

    def stack(x):
        return jnp.concatenate([jnp.where(left, x, 0.0), jnp.where(left, 0.0, x)], axis=0)

    def side_t(x):
        z = stack(x).T
        return z[:c_len] + z[c_len:]

    seqs = [(bi, p) for bi in range(n_batch) for p in range(N_PAIRS)]
    items = [(bi, p, c) for (bi, p) in seqs for c in range(chunks_per_step)]
    window = lambda bi, p, c: (bi, slice(c * c_len, (c + 1) * c_len), slice(p * PAIR, (p + 1) * PAIR))
    load = lambda ref: [ref[window(*it)] for it in items]
    each = lambda f, *cols: [f(*xs) for xs in zip(*cols)]

    lw = load(lw_ref)
    cum = each(lambda x: _dot3r(tri, x), lw)
    cum_end = each(lambda x: x[c_len - 1:c_len, :], cum)
    e_neg = each(lambda x: jnp.exp(-x), cum)
    e_tail = each(lambda x, xe: jnp.exp(xe - x), cum, cum_end)
    kk, b, k = load(kk_ref), load(b_ref), load(k_ref)
    l_a = each(lambda x, c_, w_: -x * jnp.exp(c_ - w_), kk, cum, lw)
    l_r = each(lambda x, c_: x * jnp.exp(c_), load(r_ref), cum)
    r_b = each(lambda x, e: stack(x * e), b, e_neg)
    r_k = each(lambda x, e: stack(x * e), k, e_neg)
    rb_t = each(lambda x, e: side_t(x * e), b, e_tail)
    rk_t = each(lambda x, e: side_t(x * e), k, e_tail)
    v_s = each(stack, load(v_ref))

    cat = lambda *xs: jnp.concatenate(xs, axis=0)
    part = lambda x, i: x[i * c_len:(i + 1) * c_len]
    l_ar = each(cat, l_a, l_r)
    a_b = each(_dot_nt, l_ar, r_b)
    a_k = each(_dot_nt, l_ar, r_k)
    a_ab = each(lambda a: jnp.where(strict, part(a, 0), 0.0), a_b)
    a_rb = each(lambda a: jnp.where(incl, part(a, 1), 0.0), a_b)
    a_ak = each(lambda a: jnp.where(strict, part(a, 0), 0.0), a_k)
    a_rk = each(lambda a: jnp.where(incl, part(a, 1), 0.0), a_k)

    t_inv = each(lambda a: eye + a, a_ab)
    x = each(lambda a: _dot(a, stack(a)), a_ab)
    for _ in range(4):
        xt = each(lambda x_, t: _dot(cat(x_, t), stack(x_)), x, t_inv)
        x = each(lambda p_: part(p_, 0), xt)
        t_inv = each(lambda t, p_: t + part(p_, 1), t_inv, xt)
    t_inv = each(lambda t, x_: t + _dot(t, stack(x_)), t_inv, x)

    t_a = each(lambda t, l: stack(_dot(t, stack(l))), t_inv, l_a)
    av = each(lambda a, a2, kt, v: _dot(cat(a, a2, kt), v), a_ak, a_rk, rk_t, v_s)
    w_ = each(lambda t, p_: stack(_dot(t, stack(part(p_, 0)))), t_inv, av)
    on_ta = each(lambda a, bt, t: _dot(cat(a, bt), t), a_rb, rb_t, t_a)
    on_w = each(lambda a, bt, w: _dot(cat(a, bt), w), a_rb, rb_t, w_)
    r_q = each(lambda l, p_: l + part(p_, 0), l_r, on_ta)
    m_ = each(lambda ce, p_: jnp.where(diag, jnp.exp(ce), 0.0) + part(p_, 1), cum_end, on_ta)
    y_0 = each(lambda p_, q_: part(p_, 0) + part(q_, 1), on_w, av)
    n_ = each(lambda p_, q_: part(p_, 1) + part(q_, 2), on_w, av)

    for s, (bi, p) in enumerate(seqs):
        h = h_ref[s]
        for c in range(chunks_per_step):
            idx = s * chunks_per_step + c
            on_h = _dot(cat(r_q[idx], m_[idx]), stack(h))
            y_ref[window(bi, p, c)] = part(on_h, 0) + y_0[idx]
            h = part(on_h, 1) + n_[idx]
        h_ref[s] = h


def _rwkv_scan(seq_len, r, lw, k, v, kk, b, chunks_per_step=2):
    n = r.shape[0]
    n_batch = n // seq_len
    rows = chunks_per_step * SCAN_CHUNK
    spec = pl.BlockSpec((n_batch, rows, MIX_W), lambda j: (0, j, 0))
    args = [a.reshape(n_batch, seq_len, MIX_W) for a in (r, lw, k, v, kk, b)]
    y = pl.pallas_call(
        functools.partial(_scan_body, chunks_per_step),
        grid=(seq_len // rows,),
        in_specs=[spec] * 6,
        out_specs=spec,
        out_shape=jax.ShapeDtypeStruct((n_batch, seq_len, MIX_W), F32),
        scratch_shapes=[pltpu.VMEM((n_batch * N_PAIRS, HEAD_DIM, PAIR), F32)],
        compiler_params=_params("arbitrary"),
        name="rwkv_scan",
    )(*args)
    return y.reshape(n, MIX_W)


def _fox_features(q, k, v, ff, qg, kg, fb):
    tm = q.shape[0]

    def qk_norm(x, g):
        ms = _head_sums(x * x) * (1.0 / HEAD_DIM)
        return x * lax.rsqrt(ms + RMS_EPS) * g

    lane = lax.broadcasted_iota(jnp.int32, (1, PAIR), 1)
    left = lane < HEAD_DIM

    def pad_heads(x, aug):
        cols = []
        for h in range(N_HEADS):
            src = x[:, (h // 2) * PAIR:(h // 2 + 1) * PAIR]
            if h % 2:
                src = pltpu.roll(src, HEAD_DIM, axis=1)
            cols.append(jnp.where(left, src, aug(h)))
        return jnp.concatenate(cols, axis=1).astype(BF16)

    log_f = -_softplus(-(ff + fb))
    local = _dot3r(_tri(tm, True, BF16), log_f)
    bias = local * (-LOG2E)

    def k_aug(h):
        col = jnp.sum(jnp.where(lane == h, bias, 0.0), axis=1, keepdims=True)
        hi, mid, lo = (p.astype(F32) for p in _split3(col))
        return jnp.where(lane == HEAD_DIM, hi,
                         jnp.where(lane == HEAD_DIM + 1, mid, jnp.where(lane == HEAD_DIM + 2, lo, 0.0)))

    q_aug = jnp.where((lane >= HEAD_DIM) & (lane < HEAD_DIM + 3), 1.0, 0.0)
    q_a = pad_heads(qk_norm(q, qg) * (LOG2E * HEAD_DIM ** -0.5), lambda h: q_aug)
    k_a = pad_heads(qk_norm(k, kg), k_aug)

    v_t = v.T
    ones_row = jnp.where(lax.broadcasted_iota(jnp.int32, (HEAD_DIM, tm), 0) == 0, 1.0, 0.0)
    rows = []
    for h in range(N_HEADS):
        rows += [v_t[h * HEAD_DIM:(h + 1) * HEAD_DIM, :], ones_row]
    return q_a, k_a, jnp.concatenate(rows, axis=0).astype(BF16), local


def _proj_body(seq_len, has_vres, *refs):
    n_in = 14 if has_vres else 11
    x_ref, g_ref, w_ref, mu_ref, vec_ref, w2_ref, a2_ref, g2_ref, qg_ref, kg_ref, fb_ref = refs[:11]
    vres_refs = refs[11:14] if has_vres else None
    (r_o, k_o, v_o, lw_o, kk_o, b_o, g_o, bonus_o,
     qa_o, ka_o, vt_o, cend_o, fg_o, pool_o) = refs[n_in:n_in + 14]
    prev_ref, carry_ref = refs[n_in + 14:]
    tm = x_ref.shape[0]
    blocks = seq_len // tm
    j = lax.rem(pl.program_id(0), blocks)

    @pl.when(j == 0)
    def _():
        prev_ref[...] = jnp.zeros_like(prev_ref)
        carry_ref[...] = jnp.zeros_like(carry_ref)
        cend_o[...] = jnp.zeros_like(cend_o)

    h = _rms(x_ref[...], g_ref[...]).astype(BF16)
    feats, fox, ff, pool = (jnp.dot(h, w_ref[:, c0:c1], preferred_element_type=F32) for c0, c1 in IN_SPLITS)
    pool_o[...] = pool
    fg_o[...] = fox[:, 3 * MIX_W:]

    halo = prev_ref[7:8, :]
    prev_ref[...] = feats[tm - 8:, :]
    vres = None if vres_refs is None else tuple(ref[...] for ref in vres_refs)
    outs = _rwkv_features(feats, halo, mu_ref[...], vec_ref[...], w2_ref[...], a2_ref[...], g2_ref[...], vres)
    for o_ref, val in zip((r_o, k_o, v_o, lw_o, kk_o, b_o, g_o, bonus_o), outs):
        o_ref[...] = val

    q_a, k_a, v_t, local = _fox_features(fox[:, :MIX_W], fox[:, MIX_W:2 * MIX_W], fox[:, 2 * MIX_W:3 * MIX_W],
                                         ff, qg_ref[...], kg_ref[...], fb_ref[...])
    qa_o[...] = q_a
    ka_o[...] = k_a
    vt_o[0] = v_t
    cum = local + carry_ref[...]
    carry_ref[...] = cum[tm - 1:tm, :]
    blk = lax.broadcasted_iota(jnp.int32, (1, blocks), 1)
    cend_o[0] = jnp.where(blk == j, cum.T[0:8, tm - 1:tm], cend_o[0])


def _proj(seq_len, x, g, w, mu, vecs, w2p, a2p, g2p, qg, kg, fb, vres, tm):
    n = x.shape[0]
    blocks = seq_len // tm
    full = lambda a: pl.BlockSpec(a.shape, lambda i: (0,) * a.ndim)
    tile = lambda c: pl.BlockSpec((tm, c), lambda i: (i, 0))
    in_specs = [tile(D_MODEL)] + [full(a) for a in (g, w, mu, vecs, w2p, a2p, g2p, qg, kg, fb)]
    args = [x, g, w, mu, vecs, w2p, a2p, g2p, qg, kg, fb]
    if vres is not None:
        v_first, v1p, v2p = vres
        in_specs += [tile(MIX_W), full(v1p), full(v2p)]
        args += [v_first, v1p, v2p]
    padded = N_HEADS * PAIR
    return pl.pallas_call(
        functools.partial(_proj_body, seq_len, vres is not None),
        grid=(n // tm,),
        in_specs=in_specs,
        out_specs=[tile(MIX_W)] * 8 + [tile(padded), tile(padded),
                                       pl.BlockSpec((1, padded, tm), lambda i: (i, 0, 0)),
                                       pl.BlockSpec((1, 8, blocks), lambda i: (i // blocks, 0, 0)),
                                       tile(MIX_W), tile(POOL_W)],
        out_shape=[jax.ShapeDtypeStruct((n, MIX_W), F32)] * 8
        + [jax.ShapeDtypeStruct((n, padded), BF16)] * 2
        + [jax.ShapeDtypeStruct((n // tm, padded, tm), BF16),
           jax.ShapeDtypeStruct((n // seq_len, 8, blocks), F32),
           jax.ShapeDtypeStruct((n, MIX_W), F32), jax.ShapeDtypeStruct((n, POOL_W), F32)],
        scratch_shapes=[pltpu.VMEM((8, RWKV_PAD), F32), pltpu.VMEM((1, FF_PAD), F32)],
        compiler_params=_params("arbitrary"),
        name="proj",
    )(*args)


def _fox_attn_body(q_ref, k_ref, vt_ref, cend_ref, qg_ref, kg_ref, o_ref):
    h = pl.program_id(1)
    tq = q_ref.shape[0]
    tk = vt_ref.shape[2]
    q_blocks = tq // tk
    diag = pl.program_id(2) * q_blocks
    nk = cend_ref.shape[2]
    q = q_ref[...]
    sub = lax.broadcasted_iota(jnp.int32, (8, 1), 0)
    blk = lax.broadcasted_iota(jnp.int32, (1, nk), 1)
    cend = jnp.sum(jnp.where(sub == h, cend_ref[0], 0.0), axis=0, keepdims=True)
    end_of = lambda b: jnp.sum(jnp.where(blk == b, cend, 0.0), axis=1, keepdims=True)
    c_ref = end_of(diag - 1)

    amax = lambda r: jnp.max(jnp.abs(r[...]), axis=1, keepdims=True)
    qk_bound = 8.0 * BF16_MARGIN * amax(qg_ref) * amax(kg_ref)
    needed = (blk < diag) & (c_ref - cend > -(2.0 * qk_bound + UNDERFLOW_CUT))
    first = diag - jnp.sum(jnp.where(needed, 1, 0))

    def block(j, width, carry, diagonal):
        m, acc = carry
        start = pl.multiple_of(j * tk, tk)
        z = lax.dot_general(k_ref[pl.ds(start, width * tk), :], q, (((1,), (1,)), ((), ())),
                            preferred_element_type=F32)
        if diagonal:
            key = lax.broadcasted_iota(jnp.int32, z.shape, 0)
            qry = lax.broadcasted_iota(jnp.int32, z.shape, 1)
            z = jnp.where(key <= qry, z, -1e30)
        for u in range(width):
            z_u = z[u * tk:(u + 1) * tk, :]
            s_u = (end_of(j + u - 1) - c_ref) * LOG2E
            m_new = jnp.maximum(m, jnp.max(z_u, axis=0, keepdims=True) - s_u)
            p_u = jnp.exp2(z_u - (m_new + s_u)).astype(BF16)
            acc = acc * jnp.exp2(m - m_new) + jnp.dot(vt_ref[j + u], p_u, preferred_element_type=F32)
            m = m_new
        return m, acc

    carry = (jnp.full((1, tq), -1e30, F32), jnp.zeros((PAIR, tq), F32))
    carry = block(diag, q_blocks, carry, True)
    i = diag
    wide = jnp.minimum((i - first + FOX_WIDE - 1) // FOX_WIDE, i // FOX_WIDE)
    lo = i - wide * FOX_WIDE
    carry = lax.fori_loop(0, wide, lambda g, c: block(lo + g * FOX_WIDE, FOX_WIDE, c, False), carry)
    carry = lax.fori_loop(first, lo, lambda j, c: block(j, 1, c, False), carry)
    acc = carry[1]
    value_row = lax.broadcasted_iota(jnp.int32, (PAIR, 1), 0) < HEAD_DIM
    out_t = jnp.where(value_row, acc * (1.0 / acc[HEAD_DIM:HEAD_DIM + 1, :]), 0.0)
    o_ref[...] = out_t.T


def _fox_attn(seq_len, q, k, vt, cend, qg, kg, tk, q_blocks):
    n = q.shape[0]
    tq = tk * q_blocks
    nq, nk = seq_len // tq, seq_len // tk
    full = lambda a: pl.BlockSpec(a.shape, lambda bi, h, i: (0,) * a.ndim)
    return pl.pallas_call(
        _fox_attn_body,
        grid=(n // seq_len, N_HEADS, nq),
        in_specs=[pl.BlockSpec((tq, PAIR), lambda bi, h, i: (bi * nq + i, h)),
                  pl.BlockSpec((seq_len, PAIR), lambda bi, h, i: (bi, h)),
                  pl.BlockSpec((nk, PAIR, tk), lambda bi, h, i: (bi, h, 0)),
                  pl.BlockSpec((1, 8, nk), lambda bi, h, i: (bi, 0, 0)),
                  full(qg), full(kg)],
        out_specs=pl.BlockSpec((tq, PAIR), lambda bi, h, i: (bi * nq + i, h)),
        out_shape=jax.ShapeDtypeStruct((n, N_HEADS * PAIR), F32),
        compiler_params=_params("parallel", "parallel", "arbitrary"),
        name="fox_attn",
    )(q, k, vt, cend, qg, kg)


def _mix_out_body(seq_len, x_ref, ys_ref, g_ref, bonus_ref, at_ref, fg_ref, pool_ref, halo_ref,
                  ln_ref, pw_ref, ps_ref, wo_ref, pg_ref, o_ref):
    tm = x_ref.shape[0]
    row0 = lax.rem(pl.program_id(0) * tm, seq_len)

    y = ys_ref[...]
    d = y - _head_sums(y) * (1.0 / HEAD_DIM)
    var = _head_sums(d * d) * (1.0 / HEAD_DIM)
    y_rwkv = (d * lax.rsqrt(var + GN_EPS) * ln_ref[0:1, :] + ln_ref[1:2, :] + bonus_ref[...]) * g_ref[...]

    left = lax.broadcasted_iota(jnp.int32, (1, PAIR), 1) < HEAD_DIM
    at = at_ref[...]
    packed = [jnp.where(left, at[:, 2 * p * PAIR:(2 * p + 1) * PAIR],
                        pltpu.roll(at[:, (2 * p + 1) * PAIR:(2 * p + 2) * PAIR], HEAD_DIM, axis=1))
              for p in range(N_PAIRS)]
    y_fox = jnp.concatenate(packed, axis=1) * _sigmoid(fg_ref[...])

    pin = pool_ref[...]
    halo = jnp.where(row0 == 0, 0.0, halo_ref[...])
    s = jnp.concatenate([halo, pin], axis=0)
    sums = []
    for shift in (1, 2, 4, 8):
        s = s + pltpu.roll(s, shift, axis=0)
        sums.append(s[POOL_HALO:, :])
    grp = lax.broadcasted_iota(jnp.int32, (1, POOL_W), 1) // HEAD_DIM
    win_sum = jnp.where(grp == 0, sums[0], jnp.where(grp == 1, sums[1], jnp.where(grp == 2, sums[2], sums[3])))
    win = jnp.where(grp == 0, 2.0, jnp.where(grp == 1, 4.0, jnp.where(grp == 2, 8.0, 16.0)))
    pos = (row0 + lax.broadcasted_iota(jnp.int32, (tm, 1), 0) + 1).astype(F32)
    u = win_sum / jnp.minimum(pos, win) - pin
    y_pool = _dot(u, pw_ref[...]) * ps_ref[...]

    mixed = (_dot(y_rwkv, wo_ref[0:384, :]) + _dot(y_fox, wo_ref[384:768, :])
             + _dot(y_pool, wo_ref[768:1024, :]))
    o_ref[...] = x_ref[...] + _rms(mixed, pg_ref[...])


def _mix_out(seq_len, x, y_scan, g, bonus, attn, fg, pool_in, ln, pw, ps, wo, pg, tm=512):
    n = x.shape[0]
    full = lambda a: pl.BlockSpec(a.shape, lambda i: (0,) * a.ndim)
    tile = pl.BlockSpec((tm, MIX_W), lambda i: (i, 0))
    return pl.pallas_call(
        functools.partial(_mix_out_body, seq_len),
        grid=(n // tm,),
        in_specs=[pl.BlockSpec((tm, D_MODEL), lambda i: (i, 0)), tile, tile, tile,
                  pl.BlockSpec((tm, N_HEADS * PAIR), lambda i: (i, 0)), tile,
                  pl.BlockSpec((tm, POOL_W), lambda i: (i, 0)),
                  pl.BlockSpec((POOL_HALO, POOL_W),
                               lambda i: (jnp.maximum(i * (tm // POOL_HALO) - 1, 0), 0)),
                  full(ln), full(pw), full(ps), full(wo), full(pg)],
        out_specs=pl.BlockSpec((tm, D_MODEL), lambda i: (i, 0)),
        out_shape=jax.ShapeDtypeStruct((n, D_MODEL), F32),
        compiler_params=_params("parallel"),
        name="mix_out",
    )(x, y_scan, g, bonus, attn, fg, pool_in, pool_in, ln, pw, ps, wo, pg)


def _ffn_body(x_ref, g1_ref, wgu_ref, wd_ref, g2_ref, o_ref):
    x = x_ref[...]
    h = _rms(x, g1_ref[...]).astype(BF16)
    acc = jnp.zeros(x.shape, F32)
    for c in range(FFN_HIDDEN // FFN_CHUNK):
        c0 = c * FFN_CHUNK
        gate = jnp.dot(h, wgu_ref[:, c0:c0 + FFN_CHUNK], preferred_element_type=F32)
        up = jnp.dot(h, wgu_ref[:, FFN_HIDDEN + c0:FFN_HIDDEN + c0 + FFN_CHUNK],
                     preferred_element_type=F32)
        act = (gate * _sigmoid(gate) * up).astype(BF16)
        acc = acc + jnp.dot(act, wd_ref[c0:c0 + FFN_CHUNK, :], preferred_element_type=F32)
    o_ref[...] = x + _rms(acc, g2_ref[...])


def _ffn(x, g1, wgu, wd, g2, tm=512):
    n = x.shape[0]
    full = lambda a: pl.BlockSpec(a.shape, lambda i: (0,) * a.ndim)
    tile = pl.BlockSpec((tm, D_MODEL), lambda i: (i, 0))
    return pl.pallas_call(
        _ffn_body,
        grid=(n // tm,),
        in_specs=[tile, full(g1), full(wgu), full(wd), full(g2)],
        out_specs=tile,
        out_shape=jax.ShapeDtypeStruct((n, D_MODEL), F32),
        compiler_params=_params("parallel"),
        name="ffn",
    )(x, g1, wgu, wd, g2)


def _pad_rows(a, rows, at=0):
    out = jnp.zeros((rows, a.shape[1]), a.dtype)
    return lax.dynamic_update_slice(out, a, (at, 0))


def kernel(x, mix_pre_g, mix_post_g, ffn_pre_g, ffn_post_g, w_in, w_out, rwkv_mu, rwkv_w0, rwkv_w2,
           rwkv_a0, rwkv_a2, rwkv_g2, rwkv_v0, rwkv_v1, rwkv_v2, rwkv_k_k, rwkv_k_a, rwkv_r_k,
           rwkv_ln_w, rwkv_ln_b, fox_q_g, fox_k_g, fox_f_b, pool_w, pool_scale, ffn_w_gu, ffn_w_down):
    batch, seq_len, _ = x.shape
    n = batch * seq_len
    depth = w_in.shape[0]
    row = lambda a: a.reshape(1, -1).astype(F32)
    xf = x.reshape(n, D_MODEL)
    w_arr = _w_regroup(w_in)
    v_first = None
    for l in range(depth):
        mu = jnp.pad(rwkv_mu[l], (0, RWKV_PAD - RWKV_IN)).reshape(1, RWKV_PAD)
        v0 = rwkv_v0[l - 1] if l > 0 else jnp.zeros((MIX_W,), F32)
        vecs = jnp.stack([rwkv_w0[l], rwkv_a0[l], rwkv_k_k[l], rwkv_k_a[l], rwkv_r_k[l].reshape(-1), v0,
                          jnp.zeros((MIX_W,), F32), jnp.zeros((MIX_W,), F32)])
        w2p = _pad_rows(rwkv_w2[l], 128, 0).astype(BF16)
        a2p = _pad_rows(rwkv_a2[l], 128, 64).astype(BF16)
        g2p = _pad_rows(rwkv_g2[l], 256, 0).astype(BF16)
        vres = None
        if l > 0:
            v1p = jnp.pad(rwkv_v1[l - 1], ((0, 0), (0, 128 - 32))).astype(BF16)
            v2p = _pad_rows(rwkv_v2[l - 1], 128, 0).astype(BF16)
            vres = (v_first, v1p, v2p)
        tile6 = lambda a: jnp.tile(a, N_HEADS).reshape(1, MIX_W)
        fb = jnp.pad(fox_f_b[l], (0, FF_PAD - N_HEADS)).reshape(1, FF_PAD)
        qg, kg = tile6(fox_q_g[l]), tile6(fox_k_g[l])
        (r, k, v, lw, kk, b, g, bonus, q_a, k_a, v_t, cend, fg, pool_in) = _proj(
            seq_len, xf, row(mix_pre_g[l]), w_arr[l], mu, vecs, w2p, a2p, g2p, qg, kg, fb, vres, FOX_BLOCK)
        if l == 0:
            v_first = v
        y_scan = _rwkv_scan(seq_len, r, lw, k, v, kk, b)
        attn = _fox_attn(seq_len, q_a, k_a, v_t, cend, qg, kg, FOX_BLOCK, FOX_Q_BLOCKS)

        ln = jnp.stack([rwkv_ln_w[l], rwkv_ln_b[l]] + [jnp.zeros((MIX_W,), F32)] * 6)
        pw = jax.scipy.linalg.block_diag(*[pool_w[l, gi] for gi in range(4)]).astype(BF16)
        xf = _mix_out(seq_len, xf, y_scan, g, bonus, attn, fg, pool_in, ln, pw,
                      row(pool_scale[l]), w_out[l].astype(BF16), row(mix_post_g[l]))
        xf = _ffn(xf, row(ffn_pre_g[l]), ffn_w_gu[l].astype(BF16), ffn_w_down[l].astype(BF16),
                  row(ffn_post_g[l]))
    return xf.reshape(batch, seq_len, D_MODEL)
```

```python
import functools

import jax
import jax.numpy as jnp
from jax import lax
from jax.experimental import pallas as pl
from jax.experimental.pallas import tpu as pltpu

F32 = jnp.float32
BF16 = jnp.bfloat16

D_MODEL = 1024
HEAD_DIM = 64
PAIR = 2 * HEAD_DIM
N_HEADS = 6
N_PAIRS = N_HEADS // 2
MIX_W = N_HEADS * HEAD_DIM
POOL_W = 256
POOL_HALO = 16
FFN_HIDDEN = 2816
FFN_CHUNK = 256
RMS_EPS = 1e-6
GN_EPS = HEAD_DIM * 1e-5

RWKV_IN = 1440
RWKV_PAD = 1536
FOX_IN = 4 * MIX_W
FF_PAD = 128
IN_SPLITS = ((0, 1536), (1536, 3072), (3072, 3200), (3200, 3456))
IN_TOTAL = 3456

LOG2E = 1.4426950408889634
FOX_BLOCK = 256
FOX_Q_BLOCKS = 2
FOX_WIDE = 8
UNDERFLOW_CUT = 110.0
BF16_MARGIN = 1.02
SCAN_CHUNK = 64
VMEM_LIMIT = 56 * 1024 * 1024


def _params(*sem):
    return pltpu.CompilerParams(dimension_semantics=sem, vmem_limit_bytes=VMEM_LIMIT)


def _dot(a, b):
    return jnp.dot(a.astype(BF16), b.astype(BF16), preferred_element_type=F32)


def _dot_nt(a, b):
    return lax.dot_general(a.astype(BF16), b.astype(BF16), (((1,), (1,)), ((), ())),
                           preferred_element_type=F32)


def _split3(x):
    hi = x.astype(BF16)
    r1 = x - hi.astype(F32)
    mid = r1.astype(BF16)
    lo = (r1 - mid.astype(F32)).astype(BF16)
    return hi, mid, lo


def _head_sums(x):
    left = lax.broadcasted_iota(jnp.int32, (1, PAIR), 1) < HEAD_DIM
    outs = []
    for p in range(x.shape[1] // PAIR):
        xp = x[:, p * PAIR:(p + 1) * PAIR]
        lsum = jnp.sum(jnp.where(left, xp, 0.0), axis=1, keepdims=True)
        rsum = jnp.sum(jnp.where(left, 0.0, xp), axis=1, keepdims=True)
        outs.append(jnp.where(left, lsum, rsum))
    return jnp.concatenate(outs, axis=1)


def _dot3r(w, x):
    hi, mid, lo = _split3(x)
    d = lambda t: jnp.dot(w, t, preferred_element_type=F32)
    return d(hi) + d(mid) + d(lo)


def _rms(x, g):
    ms = jnp.mean(x * x, axis=-1, keepdims=True)
    return x * lax.rsqrt(ms + RMS_EPS) * g


def _softplus(z):
    return jnp.maximum(z, 0.0) + jnp.log(1.0 + jnp.exp(-jnp.abs(z)))


def _sigmoid(z):
    return 1.0 / (1.0 + jnp.exp(-z))


def _tri(n, inclusive, dtype):
    r = lax.broadcasted_iota(jnp.int32, (n, n), 0)
    c = lax.broadcasted_iota(jnp.int32, (n, n), 1)
    keep = (r >= c) if inclusive else (r > c)
    return jnp.where(keep, 1.0, 0.0).astype(dtype)


def _w_regroup_body(w_ref, o_ref):
    w = w_ref[0]
    zeros = lambda c: jnp.zeros((w.shape[0], c), F32)
    fox0, ff0, pool0 = RWKV_IN, RWKV_IN + FOX_IN, RWKV_IN + FOX_IN + N_HEADS
    o_ref[0] = jnp.concatenate(
        [w[:, :fox0], zeros(RWKV_PAD - RWKV_IN), w[:, fox0:ff0], w[:, ff0:pool0],
         zeros(FF_PAD - N_HEADS), w[:, pool0:]], axis=1).astype(BF16)


def _w_regroup(w_in, rows=128):
    depth, d, cols = w_in.shape
    return pl.pallas_call(
        _w_regroup_body,
        grid=(depth, d // rows),
        in_specs=[pl.BlockSpec((1, rows, cols), lambda l, i: (l, i, 0))],
        out_specs=pl.BlockSpec((1, rows, IN_TOTAL), lambda l, i: (l, i, 0)),
        out_shape=jax.ShapeDtypeStruct((depth, d, IN_TOTAL), BF16),
        compiler_params=_params("parallel", "parallel"),
        name="w_regroup",
    )(w_in)


def _rwkv_features(f, halo, mu, vecs, w2, a2, g2, vres):
    tm = f.shape[0]
    row = lax.broadcasted_iota(jnp.int32, (tm, 1), 0)
    prev = jnp.where(row == 0, halo, pltpu.roll(f, 1, axis=0))
    f = f + (prev - f) * mu
    r, k, v = f[:, 0:384], f[:, 384:768], f[:, 768:1152]
    wa = f[:, 1152:1280]
    gd = f[:, 1280:1536]
    w0, a0, k_k, k_a, r_k, v0 = (vecs[j:j + 1, :] for j in range(6))

    w_log = -_softplus(-(w0 + _dot(jnp.tanh(wa), w2))) - 0.5
    lw = -jnp.exp(w_log)
    a = _sigmoid(a0 + _dot(wa, a2))
    g = _dot(_sigmoid(gd), g2)
    if vres is not None:
        v_first, v1, v2 = vres
        v = v + (v_first - v) * _sigmoid(v0 + _dot(_dot(v, v1), v2))
    kk = k * k_k
    kk = kk * lax.rsqrt(jnp.maximum(_head_sums(kk * kk), 1e-24))
    k = k * (1.0 + (a - 1.0) * k_a)
    return r, k, v, lw, kk, kk * a, g, _head_sums(r * k * r_k) * v


def _scan_body(chunks_per_step, r_ref, lw_ref, k_ref, v_ref, kk_ref, b_ref, y_ref, h_ref):
    c_len = SCAN_CHUNK
    n_batch = r_ref.shape[0]

    @pl.when(pl.program_id(0) == 0)
    def _():
        h_ref[...] = jnp.zeros_like(h_ref)

    lane = lax.broadcasted_iota(jnp.int32, (1, PAIR), 1)
    left = lane < HEAD_DIM
    ri = lax.broadcasted_iota(jnp.int32, (c_len, PAIR), 0)
    ci = lax.broadcasted_iota(jnp.int32, (c_len, PAIR), 1) & (HEAD_DIM - 1)
    strict, incl, diag = ri > ci, ri >= ci, ri == ci
    tri = _tri(c_len, True, BF16)
    eye = jnp.where(diag, 1.0, 0.0)

    def stack(x):
        return jnp.concatenate([jnp.where(left, x, 0.0), jnp.where(left, 0.0, x)], axis=0)

    def side_t(x):
        z = stack(x).T
        return z[:c_len] + z[c_len:]

    seqs = [(bi, p) for bi in range(n_batch) for p in range(N_PAIRS)]
    items = [(bi, p, c) for (bi, p) in seqs for c in range(chunks_per_step)]
    window = lambda bi, p, c: (bi, slice(c * c_len, (c + 1) * c_len), slice(p * PAIR, (p + 1) * PAIR))
    load = lambda ref: [ref[window(*it)] for it in items]
    each = lambda f, *cols: [f(*xs) for xs in zip(*cols)]

    lw = load(lw_ref)
    cum = each(lambda x: _dot3r(tri, x), lw)
    cum_end = each(lambda x: x[c_len - 1:c_len, :], cum)
    e_neg = each(lambda x: jnp.exp(-x), cum)
    e_tail = each(lambda x, xe: jnp.exp(xe - x), cum, cum_end)
    kk, b, k = load(kk_ref), load(b_ref), load(k_ref)
    l_a = each(lambda x, c_, w_: -x * jnp.exp(c_ - w_), kk, cum, lw)
    l_r = each(lambda x, c_: x * jnp.exp(c_), load(r_ref), cum)
    r_b = each(lambda x, e: stack(x * e), b, e_neg)
    r_k = each(lambda x, e: stack(x * e), k, e_neg)
    rb_t = each(lambda x, e: side_t(x * e), b, e_tail)
    rk_t = each(lambda x, e: side_t(x * e), k, e_tail)
    v_s = each(stack, load(v_ref))

    cat = lambda *xs: jnp.concatenate(xs, axis=0)
    part = lambda x, i: x[i * c_len:(i + 1) * c_len]
    l_ar = each(cat, l_a, l_r)
    a_b = each(_dot_nt, l_ar, r_b)
    a_k = each(_dot_nt, l_ar, r_k)
    a_ab = each(lambda a: jnp.where(strict, part(a, 0), 0.0), a_b)
    a_rb = each(lambda a: jnp.where(incl, part(a, 1), 0.0), a_b)
    a_ak = each(lambda a: jnp.where(strict, part(a, 0), 0.0), a_k)
    a_rk = each(lambda a: jnp.where(incl, part(a, 1), 0.0), a_k)

    t_inv = each(lambda a: eye + a, a_ab)
    x = each(lambda a: _dot(a, stack(a)), a_ab)
    for _ in range(4):
        xt = each(lambda x_, t: _dot(cat(x_, t), stack(x_)), x, t_inv)
        x = each(lambda p_: part(p_, 0), xt)
        t_inv = each(lambda t, p_: t + part(p_, 1), t_inv, xt)
    t_inv = each(lambda t, x_: t + _dot(t, stack(x_)), t_inv, x)

    t_a = each(lambda t, l: stack(_dot(t, stack(l))), t_inv, l_a)
    av = each(lambda a, a2, kt, v: _dot(cat(a, a2, kt), v), a_ak, a_rk, rk_t, v_s)
    w_ = each(lambda t, p_: stack(_dot(t, stack(part(p_, 0)))), t_inv, av)
    on_ta = each(lambda a, bt, t: _dot(cat(a, bt), t), a_rb, rb_t, t_a)
    on_w = each(lambda a, bt, w: _dot(cat(a, bt), w), a_rb, rb_t, w_)
    r_q = each(lambda l, p_: l + part(p_, 0), l_r, on_ta)
    m_ = each(lambda ce, p_: jnp.where(diag, jnp.exp(ce), 0.0) + part(p_, 1), cum_end, on_ta)
    y_0 = each(lambda p_, q_: part(p_, 0) + part(q_, 1), on_w, av)
    n_ = each(lambda p_, q_: part(p_, 1) + part(q_, 2), on_w, av)

    for s, (bi, p) in enumerate(seqs):
        h = h_ref[s]
        for c in range(chunks_per_step):
            idx = s * chunks_per_step + c
            on_h = _dot(cat(r_q[idx], m_[idx]), stack(h))
            y_ref[window(bi, p, c)] = part(on_h, 0) + y_0[idx]
            h = part(on_h, 1) + n_[idx]
        h_ref[s] = h


def _rwkv_scan(seq_len, r, lw, k, v, kk, b, chunks_per_step=2):
    n = r.shape[0]
    n_batch = n // seq_len
    rows = chunks_per_step * SCAN_CHUNK
    spec = pl.BlockSpec((n_batch, rows, MIX_W), lambda j: (0, j, 0))
    args = [a.reshape(n_batch, seq_len, MIX_W) for a in (r, lw, k, v, kk, b)]
    y = pl.pallas_call(
        functools.partial(_scan_body, chunks_per_step),
        grid=(seq_len // rows,),
        in_specs=[spec] * 6,
        out_specs=spec,
        out_shape=jax.ShapeDtypeStruct((n_batch, seq_len, MIX_W), F32),
        scratch_shapes=[pltpu.VMEM((n_batch * N_PAIRS, HEAD_DIM, PAIR), F32)],
        compiler_params=_params("arbitrary"),
        name="rwkv_scan",
    )(*args)
    return y.reshape(n, MIX_W)


def _fox_features(q, k, v, ff, qg, kg, fb):
    tm = q.shape[0]

    def qk_norm(x, g):
        ms = _head_sums(x * x) * (1.0 / HEAD_DIM)
        return x * lax.rsqrt(ms + RMS_EPS) * g

    lane = lax.broadcasted_iota(jnp.int32, (1, PAIR), 1)
    left = lane < HEAD_DIM

    def pad_heads(x, aug):
        cols = []
        for h in range(N_HEADS):
            src = x[:, (h // 2) * PAIR:(h // 2 + 1) * PAIR]
            if h % 2:
                src = pltpu.roll(src, HEAD_DIM, axis=1)
            cols.append(jnp.where(left, src, aug(h)))
        return jnp.concatenate(cols, axis=1).astype(BF16)

    log_f = -_softplus(-(ff + fb))
    local = _dot3r(_tri(tm, True, BF16), log_f)
    bias = local * (-LOG2E)

    def k_aug(h):
        col = jnp.sum(jnp.where(lane == h, bias, 0.0), axis=1, keepdims=True)
        hi, mid, lo = (p.astype(F32) for p in _split3(col))
        return jnp.where(lane == HEAD_DIM, hi,
                         jnp.where(lane == HEAD_DIM + 1, mid, jnp.where(lane == HEAD_DIM + 2, lo, 0.0)))

    q_aug = jnp.where((lane >= HEAD_DIM) & (lane < HEAD_DIM + 3), 1.0, 0.0)
    q_a = pad_heads(qk_norm(q, qg) * (LOG2E * HEAD_DIM ** -0.5), lambda h: q_aug)
    k_a = pad_heads(qk_norm(k, kg), k_aug)

    v_t = v.T
    ones_row = jnp.where(lax.broadcasted_iota(jnp.int32, (HEAD_DIM, tm), 0) == 0, 1.0, 0.0)
    rows = []
    for h in range(N_HEADS):
        rows += [v_t[h * HEAD_DIM:(h + 1) * HEAD_DIM, :], ones_row]
    return q_a, k_a, jnp.concatenate(rows, axis=0).astype(BF16), local


def _proj_body(seq_len, has_vres, *refs):
    n_in = 14 if has_vres else 11
    x_ref, g_ref, w_ref, mu_ref, vec_ref, w2_ref, a2_ref, g2_ref, qg_ref, kg_ref, fb_ref = refs[:11]
    vres_refs = refs[11:14] if has_vres else None
    (r_o, k_o, v_o, lw_o, kk_o, b_o, g_o, bonus_o,
     qa_o, ka_o, vt_o, cend_o, fg_o, pool_o) = refs[n_in:n_in + 14]
    prev_ref, carry_ref = refs[n_in + 14:]
    tm = x_ref.shape[0]
    blocks = seq_len // tm
    j = lax.rem(pl.program_id(0), blocks)

    @pl.when(j == 0)
    def _():
        prev_ref[...] = jnp.zeros_like(prev_ref)
        carry_ref[...] = jnp.zeros_like(carry_ref)
        cend_o[...] = jnp.zeros_like(cend_o)

    h = _rms(x_ref[...], g_ref[...]).astype(BF16)
    feats, fox, ff, pool = (jnp.dot(h, w_ref[:, c0:c1], preferred_element_type=F32) for c0, c1 in IN_SPLITS)
    pool_o[...] = pool
    fg_o[...] = fox[:, 3 * MIX_W:]

    halo = prev_ref[7:8, :]
    prev_ref[...] = feats[tm - 8:, :]
    vres = None if vres_refs is None else tuple(ref[...] for ref in vres_refs)
    outs = _rwkv_features(feats, halo, mu_ref[...], vec_ref[...], w2_ref[...], a2_ref[...], g2_ref[...], vres)
    for o_ref, val in zip((r_o, k_o, v_o, lw_o, kk_o, b_o, g_o, bonus_o), outs):
        o_ref[...] = val

    q_a, k_a, v_t, local = _fox_features(fox[:, :MIX_W], fox[:, MIX_W:2 * MIX_W], fox[:, 2 * MIX_W:3 * MIX_W],
                                         ff, qg_ref[...], kg_ref[...], fb_ref[...])
    qa_o[...] = q_a
    ka_o[...] = k_a
    vt_o[0] = v_t
    cum = local + carry_ref[...]
    carry_ref[...] = cum[tm - 1:tm, :]
    blk = lax.broadcasted_iota(jnp.int32, (1, blocks), 1)
    cend_o[0] = jnp.where(blk == j, cum.T[0:8, tm - 1:tm], cend_o[0])


def _proj(seq_len, x, g, w, mu, vecs, w2p, a2p, g2p, qg, kg, fb, vres, tm):
    n = x.shape[0]
    blocks = seq_len // tm
    full = lambda a: pl.BlockSpec(a.shape, lambda i: (0,) * a.ndim)
    tile = lambda c: pl.BlockSpec((tm, c), lambda i: (i, 0))
    in_specs = [tile(D_MODEL)] + [full(a) for a in (g, w, mu, vecs, w2p, a2p, g2p, qg, kg, fb)]
    args = [x, g, w, mu, vecs, w2p, a2p, g2p, qg, kg, fb]
    if vres is not None:
        v_first, v1p, v2p = vres
        in_specs += [tile(MIX_W), full(v1p), full(v2p)]
        args += [v_first, v1p, v2p]
    padded = N_HEADS * PAIR
    return pl.pallas_call(
        functools.partial(_proj_body, seq_len, vres is not None),
        grid=(n // tm,),
        in_specs=in_specs,
        out_specs=[tile(MIX_W)] * 8 + [tile(padded), tile(padded),
                                       pl.BlockSpec((1, padded, tm), lambda i: (i, 0, 0)),
                                       pl.BlockSpec((1, 8, blocks), lambda i: (i // blocks, 0, 0)),
                                       tile(MIX_W), tile(POOL_W)],
        out_shape=[jax.ShapeDtypeStruct((n, MIX_W), F32)] * 8
        + [jax.ShapeDtypeStruct((n, padded), BF16)] * 2
        + [jax.ShapeDtypeStruct((n // tm, padded, tm), BF16),
           jax.ShapeDtypeStruct((n // seq_len, 8, blocks), F32),
           jax.ShapeDtypeStruct((n, MIX_W), F32), jax.ShapeDtypeStruct((n, POOL_W), F32)],
        scratch_shapes=[pltpu.VMEM((8, RWKV_PAD), F32), pltpu.VMEM((1, FF_PAD), F32)],
        compiler_params=_params("arbitrary"),
        name="proj",
    )(*args)


def _fox_attn_body(q_ref, k_ref, vt_ref, cend_ref, qg_ref, kg_ref, o_ref):
    h = pl.program_id(1)
    tq = q_ref.shape[0]
    tk = vt_ref.shape[2]
    q_blocks = tq // tk
    diag = pl.program_id(2) * q_blocks
    nk = cend_ref.shape[2]
    q = q_ref[...]
    sub = lax.broadcasted_iota(jnp.int32, (8, 1), 0)
    blk = lax.broadcasted_iota(jnp.int32, (1, nk), 1)
    cend = jnp.sum(jnp.where(sub == h, cend_ref[0], 0.0), axis=0, keepdims=True)
    end_of = lambda b: jnp.sum(jnp.where(blk == b, cend, 0.0), axis=1, keepdims=True)
    c_ref = end_of(diag - 1)

    amax = lambda r: jnp.max(jnp.abs(r[...]), axis=1, keepdims=True)
    qk_bound = 8.0 * BF16_MARGIN * amax(qg_ref) * amax(kg_ref)
    needed = (blk < diag) & (c_ref - cend > -(2.0 * qk_bound + UNDERFLOW_CUT))
    first = diag - jnp.sum(jnp.where(needed, 1, 0))

    def block(j, width, carry, diagonal):
        m, acc = carry
        start = pl.multiple_of(j * tk, tk)
        z = lax.dot_general(k_ref[pl.ds(start, width * tk), :], q, (((1,), (1,)), ((), ())),
                            preferred_element_type=F32)
        if diagonal:
            key = lax.broadcasted_iota(jnp.int32, z.shape, 0)
            qry = lax.broadcasted_iota(jnp.int32, z.shape, 1)
            z = jnp.where(key <= qry, z, -1e30)
        for u in range(width):
            z_u = z[u * tk:(u + 1) * tk, :]
            s_u = (end_of(j + u - 1) - c_ref) * LOG2E
            m_new = jnp.maximum(m, jnp.max(z_u, axis=0, keepdims=True) - s_u)
            p_u = jnp.exp2(z_u - (m_new + s_u)).astype(BF16)
            acc = acc * jnp.exp2(m - m_new) + jnp.dot(vt_ref[j + u], p_u, preferred_element_type=F32)
            m = m_new
        return m, acc

    carry = (jnp.full((1, tq), -1e30, F32), jnp.zeros((PAIR, tq), F32))
    carry = block(diag, q_blocks, carry, True)
    i = diag
    wide = jnp.minimum((i - first + FOX_WIDE - 1) // FOX_WIDE, i // FOX_WIDE)
    lo = i - wide * FOX_WIDE
    carry = lax.fori_loop(0, wide, lambda g, c: block(lo + g * FOX_WIDE, FOX_WIDE, c, False), carry)
    carry = lax.fori_loop(first, lo, lambda j, c: block(j, 1, c, False), carry)
    acc = carry[1]
    value_row = lax.broadcasted_iota(jnp.int32, (PAIR, 1), 0) < HEAD_DIM
    out_t = jnp.where(value_row, acc * (1.0 / acc[HEAD_DIM:HEAD_DIM + 1, :]), 0.0)
    o_ref[...] = out_t.T


def _fox_attn(seq_len, q, k, vt, cend, qg, kg, tk, q_blocks):
    n = q.shape[0]
    tq = tk * q_blocks
    nq, nk = seq_len // tq, seq_len // tk
    full = lambda a: pl.BlockSpec(a.shape, lambda bi, h, i: (0,) * a.ndim)
    return pl.pallas_call(
        _fox_attn_body,
        grid=(n // seq_len, N_HEADS, nq),
        in_specs=[pl.BlockSpec((tq, PAIR), lambda bi, h, i: (bi * nq + i, h)),
                  pl.BlockSpec((seq_len, PAIR), lambda bi, h, i: (bi, h)),
                  pl.BlockSpec((nk, PAIR, tk), lambda bi, h, i: (bi, h, 0)),
                  pl.BlockSpec((1, 8, nk), lambda bi, h, i: (bi, 0, 0)),
                  full(qg), full(kg)],
        out_specs=pl.BlockSpec((tq, PAIR), lambda bi, h, i: (bi * nq + i, h)),
        out_shape=jax.ShapeDtypeStruct((n, N_HEADS * PAIR), F32),
        compiler_params=_params("parallel", "parallel", "arbitrary"),
        name="fox_attn",
    )(q, k, vt, cend, qg, kg)


def _mix_out_body(seq_len, x_ref, ys_ref, g_ref, bonus_ref, at_ref, fg_ref, pool_ref, halo_ref,
                  ln_ref, pw_ref, ps_ref, wo_ref, pg_ref, o_ref):
    tm = x_ref.shape[0]
    row0 = lax.rem(pl.program_id(0) * tm, seq_len)

    y = ys_ref[...]
    d = y - _head_sums(y) * (1.0 / HEAD_DIM)
    var = _head_sums(d * d) * (1.0 / HEAD_DIM)
    y_rwkv = (d * lax.rsqrt(var + GN_EPS) * ln_ref[0:1, :] + ln_ref[1:2, :] + bonus_ref[...]) * g_ref[...]

    left = lax.broadcasted_iota(jnp.int32, (1, PAIR), 1) < HEAD_DIM
    at = at_ref[...]
    packed = [jnp.where(left, at[:, 2 * p * PAIR:(2 * p + 1) * PAIR],
                        pltpu.roll(at[:, (2 * p + 1) * PAIR:(2 * p + 2) * PAIR], HEAD_DIM, axis=1))
              for p in range(N_PAIRS)]
    y_fox = jnp.concatenate(packed, axis=1) * _sigmoid(fg_ref[...])

    pin = pool_ref[...]
    halo = jnp.where(row0 == 0, 0.0, halo_ref[...])
    s = jnp.concatenate([halo, pin], axis=0)
    sums = []
    for shift in (1, 2, 4, 8):
        s = s + pltpu.roll(s, shift, axis=0)
        sums.append(s[POOL_HALO:, :])
    grp = lax.broadcasted_iota(jnp.int32, (1, POOL_W), 1) // HEAD_DIM
    win_sum = jnp.where(grp == 0, sums[0], jnp.where(grp == 1, sums[1], jnp.where(grp == 2, sums[2], sums[3])))
    win = jnp.where(grp == 0, 2.0, jnp.where(grp == 1, 4.0, jnp.where(grp == 2, 8.0, 16.0)))
    pos = (row0 + lax.broadcasted_iota(jnp.int32, (tm, 1), 0) + 1).astype(F32)
    u = win_sum / jnp.minimum(pos, win) - pin
    y_pool = _dot(u, pw_ref[...]) * ps_ref[...]

    mixed = (_dot(y_rwkv, wo_ref[0:384, :]) + _dot(y_fox, wo_ref[384:768, :])
             + _dot(y_pool, wo_ref[768:1024, :]))
    o_ref[...] = x_ref[...] + _rms(mixed, pg_ref[...])


def _mix_out(seq_len, x, y_scan, g, bonus, attn, fg, pool_in, ln, pw, ps, wo, pg, tm=512):
    n = x.shape[0]
    full = lambda a: pl.BlockSpec(a.shape, lambda i: (0,) * a.ndim)
    tile = pl.BlockSpec((tm, MIX_W), lambda i: (i, 0))
    return pl.pallas_call(
        functools.partial(_mix_out_body, seq_len),
        grid=(n // tm,),
        in_specs=[pl.BlockSpec((tm, D_MODEL), lambda i: (i, 0)), tile, tile, tile,
                  pl.BlockSpec((tm, N_HEADS * PAIR), lambda i: (i, 0)), tile,
                  pl.BlockSpec((tm, POOL_W), lambda i: (i, 0)),
                  pl.BlockSpec((POOL_HALO, POOL_W),
                               lambda i: (jnp.maximum(i * (tm // POOL_HALO) - 1, 0), 0)),
                  full(ln), full(pw), full(ps), full(wo), full(pg)],
        out_specs=pl.BlockSpec((tm, D_MODEL), lambda i: (i, 0)),
        out_shape=jax.ShapeDtypeStruct((n, D_MODEL), F32),
        compiler_params=_params("parallel"),
        name="mix_out",
    )(x, y_scan, g, bonus, attn, fg, pool_in, pool_in, ln, pw, ps, wo, pg)


def _ffn_body(x_ref, g1_ref, wgu_ref, wd_ref, g2_ref, o_ref):
    x = x_ref[...]
    h = _rms(x, g1_ref[...]).astype(BF16)
    acc = jnp.zeros(x.shape, F32)
    for c in range(FFN_HIDDEN // FFN_CHUNK):
        c0 = c * FFN_CHUNK
        gate = jnp.dot(h, wgu_ref[:, c0:c0 + FFN_CHUNK], preferred_element_type=F32)
        up = jnp.dot(h, wgu_ref[:, FFN_HIDDEN + c0:FFN_HIDDEN + c0 + FFN_CHUNK],
                     preferred_element_type=F32)
        act = (gate * _sigmoid(gate) * up).astype(BF16)
        acc = acc + jnp.dot(act, wd_ref[c0:c0 + FFN_CHUNK, :], preferred_element_type=F32)
    o_ref[...] = x + _rms(acc, g2_ref[...])


def _ffn(x, g1, wgu, wd, g2, tm=512):
    n = x.shape[0]
    full = lambda a: pl.BlockSpec(a.shape, lambda i: (0,) * a.ndim)
    tile = pl.BlockSpec((tm, D_MODEL), lambda i: (i, 0))
    return pl.pallas_call(
        _ffn_body,
        grid=(n // tm,),
        in_specs=[tile, full(g1), full(wgu), full(wd), full(g2)],
        out_specs=tile,
        out_shape=jax.ShapeDtypeStruct((n, D_MODEL), F32),
        compiler_params=_params("parallel"),
        name="ffn",
    )(x, g1, wgu, wd, g2)


def _pad_rows(a, rows, at=0):
    out = jnp.zeros((rows, a.shape[1]), a.dtype)
    return lax.dynamic_update_slice(out, a, (at, 0))


def kernel(x, mix_pre_g, mix_post_g, ffn_pre_g, ffn_post_g, w_in, w_out, rwkv_mu, rwkv_w0, rwkv_w2,
           rwkv_a0, rwkv_a2, rwkv_g2, rwkv_v0, rwkv_v1, rwkv_v2, rwkv_k_k, rwkv_k_a, rwkv_r_k,
           rwkv_ln_w, rwkv_ln_b, fox_q_g, fox_k_g, fox_f_b, pool_w, pool_scale, ffn_w_gu, ffn_w_down):
    batch, seq_len, _ = x.shape
    n = batch * seq_len
    depth = w_in.shape[0]
    row = lambda a: a.reshape(1, -1).astype(F32)
    xf = x.reshape(n, D_MODEL)
    w_arr = _w_regroup(w_in)
    v_first = None
    for l in range(depth):
        mu = jnp.pad(rwkv_mu[l], (0, RWKV_PAD - RWKV_IN)).reshape(1, RWKV_PAD)
        v0 = rwkv_v0[l - 1] if l > 0 else jnp.zeros((MIX_W,), F32)
        vecs = jnp.stack([rwkv_w0[l], rwkv_a0[l], rwkv_k_k[l], rwkv_k_a[l], rwkv_r_k[l].reshape(-1), v0,
                          jnp.zeros((MIX_W,), F32), jnp.zeros((MIX_W,), F32)])
        w2p = _pad_rows(rwkv_w2[l], 128, 0).astype(BF16)
        a2p = _pad_rows(rwkv_a2[l], 128, 64).astype(BF16)
        g2p = _pad_rows(rwkv_g2[l], 256, 0).astype(BF16)
        vres = None
        if l > 0:
            v1p = jnp.pad(rwkv_v1[l - 1], ((0, 0), (0, 128 - 32))).astype(BF16)
            v2p = _pad_rows(rwkv_v2[l - 1], 128, 0).astype(BF16)
            vres = (v_first, v1p, v2p)
        tile6 = lambda a: jnp.tile(a, N_HEADS).reshape(1, MIX_W)
        fb = jnp.pad(fox_f_b[l], (0, FF_PAD - N_HEADS)).reshape(1, FF_PAD)
        qg, kg = tile6(fox_q_g[l]), tile6(fox_k_g[l])
        (r, k, v, lw, kk, b, g, bonus, q_a, k_a, v_t, cend, fg, pool_in) = _proj(
            seq_len, xf, row(mix_pre_g[l]), w_arr[l], mu, vecs, w2p, a2p, g2p, qg, kg, fb, vres, FOX_BLOCK)
        if l == 0:
            v_first = v
        y_scan = _rwkv_scan(seq_len, r, lw, k, v, kk, b)
        attn = _fox_attn(seq_len, q_a, k_a, v_t, cend, qg, kg, FOX_BLOCK, FOX_Q_BLOCKS)

        ln = jnp.stack([rwkv_ln_w[l], rwkv_ln_b[l]] + [jnp.zeros((MIX_W,), F32)] * 6)
        pw = jax.scipy.linalg.block_diag(*[pool_w[l, gi] for gi in range(4)]).astype(BF16)
        xf = _mix_out(seq_len, xf, y_scan, g, bonus, attn, fg, pool_in, ln, pw,
                      row(pool_scale[l]), w_out[l].astype(BF16), row(mix_post_g[l]))
        xf = _ffn(xf, row(ffn_pre_g[l]), ffn_w_gu[l].astype(BF16), ffn_w_down[l].astype(BF16),
                  row(ffn_post_g[l]))
    return xf.reshape(batch, seq_len, D_MODEL)
```

```python
import functools

import jax
import jax.numpy as jnp
from jax import lax
from jax.experimental import pallas as pl
from jax.experimental.pallas import tpu as pltpu

F32 = jnp.float32
BF16 = jnp.bfloat16

D_MODEL = 1024
HEAD_DIM = 64
PAIR = 2 * HEAD_DIM
N_HEADS = 6
N_PAIRS = N_HEADS // 2
MIX_W = N_HEADS * HEAD_DIM
POOL_W = 256
POOL_HALO = 16
FFN_HIDDEN = 2816
FFN_CHUNK = 256
RMS_EPS = 1e-6
GN_EPS = HEAD_DIM * 1e-5

DECAY_LORA, ICLR_LORA, GATE_LORA, VRES_LORA = 64, 64, 160, 32
RWKV_IN = 3 * MIX_W + DECAY_LORA + ICLR_LORA + GATE_LORA
RWKV_PAD = 1536
FOX_IN = 4 * MIX_W
FF_PAD = 128
IN_WIDTHS = (RWKV_PAD, FOX_IN, FF_PAD, POOL_W)
IN_SPLITS = tuple((sum(IN_WIDTHS[:i]), sum(IN_WIDTHS[:i + 1])) for i in range(len(IN_WIDTHS)))
IN_TOTAL = sum(IN_WIDTHS)

LOG2E = 1.4426950408889634
FOX_BLOCK = 256
FOX_Q_BLOCKS = 2
FOX_WIDE = 8
UNDERFLOW_CUT = 110.0
BF16_MARGIN = 1.02
SCAN_CHUNK = 64
VMEM_LIMIT = 56 * 1024 * 1024


def _params(*sem):
    return pltpu.CompilerParams(dimension_semantics=sem, vmem_limit_bytes=VMEM_LIMIT)


def _dot(a, b):
    return jnp.dot(a.astype(BF16), b.astype(BF16), preferred_element_type=F32)


def _dot_nt(a, b):
    return lax.dot_general(a.astype(BF16), b.astype(BF16), (((1,), (1,)), ((), ())),
                           preferred_element_type=F32)


def _split3(x):
    hi = x.astype(BF16)
    r1 = x - hi.astype(F32)
    mid = r1.astype(BF16)
    lo = (r1 - mid.astype(F32)).astype(BF16)
    return hi, mid, lo


def _head_sums(x):
    left = lax.broadcasted_iota(jnp.int32, (1, PAIR), 1) < HEAD_DIM
    outs = []
    for p in range(x.shape[1] // PAIR):
        xp = x[:, p * PAIR:(p + 1) * PAIR]
        lsum = jnp.sum(jnp.where(left, xp, 0.0), axis=1, keepdims=True)
        rsum = jnp.sum(jnp.where(left, 0.0, xp), axis=1, keepdims=True)
        outs.append(jnp.where(left, lsum, rsum))
    return jnp.concatenate(outs, axis=1)


def _dot3r(w, x):
    hi, mid, lo = _split3(x)
    d = lambda t: jnp.dot(w, t, preferred_element_type=F32)
    return d(hi) + d(mid) + d(lo)


def _rms(x, g):
    ms = jnp.mean(x * x, axis=-1, keepdims=True)
    return x * lax.rsqrt(ms + RMS_EPS) * g


def _softplus(z):
    return jnp.maximum(z, 0.0) + jnp.log(1.0 + jnp.exp(-jnp.abs(z)))


def _sigmoid(z):
    return 1.0 / (1.0 + jnp.exp(-z))


def _tri(n, inclusive, dtype):
    r = lax.broadcasted_iota(jnp.int32, (n, n), 0)
    c = lax.broadcasted_iota(jnp.int32, (n, n), 1)
    keep = (r >= c) if inclusive else (r > c)
    return jnp.where(keep, 1.0, 0.0).astype(dtype)


def _w_regroup_body(w_ref, o_ref):
    w = w_ref[0]
    zeros = lambda c: jnp.zeros((w.shape[0], c), F32)
    fox0, ff0, pool0 = RWKV_IN, RWKV_IN + FOX_IN, RWKV_IN + FOX_IN + N_HEADS
    o_ref[0] = jnp.concatenate(
        [w[:, :fox0], zeros(RWKV_PAD - RWKV_IN), w[:, fox0:ff0], w[:, ff0:pool0],
         zeros(FF_PAD - N_HEADS), w[:, pool0:]], axis=1).astype(BF16)


def _w_regroup(w_in, rows=128):
    depth, d, cols = w_in.shape
    return pl.pallas_call(
        _w_regroup_body,
        grid=(depth, d // rows),
        in_specs=[pl.BlockSpec((1, rows, cols), lambda l, i: (l, i, 0))],
        out_specs=pl.BlockSpec((1, rows, IN_TOTAL), lambda l, i: (l, i, 0)),
        out_shape=jax.ShapeDtypeStruct((depth, d, IN_TOTAL), BF16),
        compiler_params=_params("parallel", "parallel"),
        name="w_regroup",
    )(w_in)


def _rwkv_features(f, halo, mu, vecs, w2, a2, g2, vres):
    tm = f.shape[0]
    row = lax.broadcasted_iota(jnp.int32, (tm, 1), 0)
    prev = jnp.where(row == 0, halo, pltpu.roll(f, 1, axis=0))
    f = f + (prev - f) * mu
    r, k, v = f[:, 0:MIX_W], f[:, MIX_W:2 * MIX_W], f[:, 2 * MIX_W:3 * MIX_W]
    lora0 = 3 * MIX_W
    wa = f[:, lora0:lora0 + PAIR]
    gd = f[:, lora0 + PAIR:RWKV_PAD]
    w0, a0, k_k, k_a, r_k, v0 = (vecs[j:j + 1, :] for j in range(6))

    w_log = -_softplus(-(w0 + _dot(jnp.tanh(wa), w2))) - 0.5
    lw = -jnp.exp(w_log)
    a = _sigmoid(a0 + _dot(wa, a2))
    g = _dot(_sigmoid(gd), g2)
    if vres is not None:
        v_first, v1, v2 = vres
        v = v + (v_first - v) * _sigmoid(v0 + _dot(_dot(v, v1), v2))
    kk = k * k_k
    kk = kk * lax.rsqrt(jnp.maximum(_head_sums(kk * kk), 1e-24))
    k = k * (1.0 + (a - 1.0) * k_a)
    return r, k, v, lw, kk, kk * a, g, _head_sums(r * k * r_k) * v


def _scan_body(chunks_per_step, r_ref, lw_ref, k_ref, v_ref, kk_ref, b_ref, y_ref, h_ref):
    c_len = SCAN_CHUNK
    n_batch = r_ref.shape[0]

    @pl.when(pl.program_id(0) == 0)
    def _():
        h_ref[...] = jnp.zeros_like(h_ref)

    lane = lax.broadcasted_iota(jnp.int32, (1, PAIR), 1)
    left = lane < HEAD_DIM
    ri = lax.broadcasted_iota(jnp.int32, (c_len, PAIR), 0)
    ci = lax.broadcasted_iota(jnp.int32, (c_len, PAIR), 1) & (HEAD_DIM - 1)
    strict, incl, diag = ri > ci, ri >= ci, ri == ci
    tri = _tri(c_len, True, BF16)
    eye = jnp.where(diag, 1.0, 0.0)

    def stack(x):
        return jnp.concatenate([jnp.where(left, x, 0.0), jnp.where(left, 0.0, x)], axis=0)

    def side_t(x):
        z = stack(x).T
        return z[:c_len] + z[c_len:]

    seqs = [(bi, p) for bi in range(n_batch) for p in range(N_PAIRS)]
    items = [(bi, p, c) for (bi, p) in seqs for c in range(chunks_per_step)]
    window = lambda bi, p, c: (bi, slice(c * c_len, (c + 1) * c_len), slice(p * PAIR, (p + 1) * PAIR))
    load = lambda ref: [ref[window(*it)] for it in items]
    each = lambda f, *cols: [f(*xs) for xs in zip(*cols)]

    lw = load(lw_ref)
    cum = each(lambda x: _dot3r(tri, x), lw)
    cum_end = each(lambda x: x[c_len - 1:c_len, :], cum)
    e_neg = each(lambda x: jnp.exp(-x), cum)
    e_tail = each(lambda x, xe: jnp.exp(xe - x), cum, cum_end)
    kk, b, k = load(kk_ref), load(b_ref), load(k_ref)
    l_a = each(lambda x, c_, w_: -x * jnp.exp(c_ - w_), kk, cum, lw)
    l_r = each(lambda x, c_: x * jnp.exp(c_), load(r_ref), cum)
    r_b = each(lambda x, e: stack(x * e), b, e_neg)
    r_k = each(lambda x, e: stack(x * e), k, e_neg)
    rb_t = each(lambda x, e: side_t(x * e), b, e_tail)
    rk_t = each(lambda x, e: side_t(x * e), k, e_tail)
    v_s = each(stack, load(v_ref))

    cat = lambda *xs: jnp.concatenate(xs, axis=0)
    part = lambda x, i: x[i * c_len:(i + 1) * c_len]
    l_ar = each(cat, l_a, l_r)
    a_b = each(_dot_nt, l_ar, r_b)
    a_k = each(_dot_nt, l_ar, r_k)
    a_ab = each(lambda a: jnp.where(strict, part(a, 0), 0.0), a_b)
    a_rb = each(lambda a: jnp.where(incl, part(a, 1), 0.0), a_b)
    a_ak = each(lambda a: jnp.where(strict, part(a, 0), 0.0), a_k)
    a_rk = each(lambda a: jnp.where(incl, part(a, 1), 0.0), a_k)

    t_inv = each(lambda a: eye + a, a_ab)
    x = each(lambda a: _dot(a, stack(a)), a_ab)
    for _ in range(4):
        xt = each(lambda x_, t: _dot(cat(x_, t), stack(x_)), x, t_inv)
        x = each(lambda p_: part(p_, 0), xt)
        t_inv = each(lambda t, p_: t + part(p_, 1), t_inv, xt)
    t_inv = each(lambda t, x_: t + _dot(t, stack(x_)), t_inv, x)

    t_a = each(lambda t, l: stack(_dot(t, stack(l))), t_inv, l_a)
    av = each(lambda a, a2, kt, v: _dot(cat(a, a2, kt), v), a_ak, a_rk, rk_t, v_s)
    w_ = each(lambda t, p_: stack(_dot(t, stack(part(p_, 0)))), t_inv, av)
    on_ta = each(lambda a, bt, t: _dot(cat(a, bt), t), a_rb, rb_t, t_a)
    on_w = each(lambda a, bt, w: _dot(cat(a, bt), w), a_rb, rb_t, w_)
    r_q = each(lambda l, p_: l + part(p_, 0), l_r, on_ta)
    m_ = each(lambda ce, p_: jnp.where(diag, jnp.exp(ce), 0.0) + part(p_, 1), cum_end, on_ta)
    y_0 = each(lambda p_, q_: part(p_, 0) + part(q_, 1), on_w, av)
    n_ = each(lambda p_, q_: part(p_, 1) + part(q_, 2), on_w, av)

    for s, (bi, p) in enumerate(seqs):
        h = h_ref[s]
        for c in range(chunks_per_step):
            idx = s * chunks_per_step + c
            on_h = _dot(cat(r_q[idx], m_[idx]), stack(h))
            y_ref[window(bi, p, c)] = part(on_h, 0) + y_0[idx]
            h = part(on_h, 1) + n_[idx]
        h_ref[s] = h


def _rwkv_scan(seq_len, r, lw, k, v, kk, b, chunks_per_step=2):
    n = r.shape[0]
    n_batch = n // seq_len
    rows = chunks_per_step * SCAN_CHUNK
    spec = pl.BlockSpec((n_batch, rows, MIX_W), lambda j: (0, j, 0))
    args = [a.reshape(n_batch, seq_len, MIX_W) for a in (r, lw, k, v, kk, b)]
    y = pl.pallas_call(
        functools.partial(_scan_body, chunks_per_step),
        grid=(seq_len // rows,),
        in_specs=[spec] * 6,
        out_specs=spec,
        out_shape=jax.ShapeDtypeStruct((n_batch, seq_len, MIX_W), F32),
        scratch_shapes=[pltpu.VMEM((n_batch * N_PAIRS, HEAD_DIM, PAIR), F32)],
        compiler_params=_params("arbitrary"),
        name="rwkv_scan",
    )(*args)
    return y.reshape(n, MIX_W)


def _fox_features(q, k, v, ff, qg, kg, fb):
    tm = q.shape[0]

    def qk_norm(x, g):
        ms = _head_sums(x * x) * (1.0 / HEAD_DIM)
        return x * lax.rsqrt(ms + RMS_EPS) * g

    lane = lax.broadcasted_iota(jnp.int32, (1, PAIR), 1)
    left = lane < HEAD_DIM

    def pad_heads(x, aug):
        cols = []
        for h in range(N_HEADS):
            src = x[:, (h // 2) * PAIR:(h // 2 + 1) * PAIR]
            if h % 2:
                src = pltpu.roll(src, HEAD_DIM, axis=1)
            cols.append(jnp.where(left, src, aug(h)))
        return jnp.concatenate(cols, axis=1).astype(BF16)

    log_f = -_softplus(-(ff + fb))
    local = _dot3r(_tri(tm, True, BF16), log_f)
    bias = local * (-LOG2E)

    def k_aug(h):
        col = jnp.sum(jnp.where(lane == h, bias, 0.0), axis=1, keepdims=True)
        hi, mid, lo = (p.astype(F32) for p in _split3(col))
        return jnp.where(lane == HEAD_DIM, hi,
                         jnp.where(lane == HEAD_DIM + 1, mid, jnp.where(lane == HEAD_DIM + 2, lo, 0.0)))

    q_aug = jnp.where((lane >= HEAD_DIM) & (lane < HEAD_DIM + 3), 1.0, 0.0)
    q_a = pad_heads(qk_norm(q, qg) * (LOG2E * HEAD_DIM ** -0.5), lambda h: q_aug)
    k_a = pad_heads(qk_norm(k, kg), k_aug)

    v_t = v.T
    ones_row = jnp.where(lax.broadcasted_iota(jnp.int32, (HEAD_DIM, tm), 0) == 0, 1.0, 0.0)
    rows = []
    for h in range(N_HEADS):
        rows += [v_t[h * HEAD_DIM:(h + 1) * HEAD_DIM, :], ones_row]
    return q_a, k_a, jnp.concatenate(rows, axis=0).astype(BF16), local


def _proj_body(seq_len, has_vres, *refs):
    n_in = 14 if has_vres else 11
    x_ref, g_ref, w_ref, mu_ref, vec_ref, w2_ref, a2_ref, g2_ref, qg_ref, kg_ref, fb_ref = refs[:11]
    vres_refs = refs[11:14] if has_vres else None
    (r_o, k_o, v_o, lw_o, kk_o, b_o, g_o, bonus_o,
     qa_o, ka_o, vt_o, cend_o, fg_o, pool_o) = refs[n_in:n_in + 14]
    prev_ref, carry_ref = refs[n_in + 14:]
    tm = x_ref.shape[0]
    blocks = seq_len // tm
    j = lax.rem(pl.program_id(0), blocks)

    @pl.when(j == 0)
    def _():
        prev_ref[...] = jnp.zeros_like(prev_ref)
        carry_ref[...] = jnp.zeros_like(carry_ref)
        cend_o[...] = jnp.zeros_like(cend_o)

    h = _rms(x_ref[...], g_ref[...]).astype(BF16)
    feats, fox, ff, pool = (jnp.dot(h, w_ref[:, c0:c1], preferred_element_type=F32) for c0, c1 in IN_SPLITS)
    pool_o[...] = pool
    fg_o[...] = fox[:, 3 * MIX_W:]

    halo = prev_ref[7:8, :]
    prev_ref[...] = feats[tm - 8:, :]
    vres = None if vres_refs is None else tuple(ref[...] for ref in vres_refs)
    outs = _rwkv_features(feats, halo, mu_ref[...], vec_ref[...], w2_ref[...], a2_ref[...], g2_ref[...], vres)
    for o_ref, val in zip((r_o, k_o, v_o, lw_o, kk_o, b_o, g_o, bonus_o), outs):
        o_ref[...] = val

    q_a, k_a, v_t, local = _fox_features(fox[:, :MIX_W], fox[:, MIX_W:2 * MIX_W], fox[:, 2 * MIX_W:3 * MIX_W],
                                         ff, qg_ref[...], kg_ref[...], fb_ref[...])
    qa_o[...] = q_a
    ka_o[...] = k_a
    vt_o[0] = v_t
    cum = local + carry_ref[...]
    carry_ref[...] = cum[tm - 1:tm, :]
    blk = lax.broadcasted_iota(jnp.int32, (1, blocks), 1)
    cend_o[0] = jnp.where(blk == j, cum.T[0:8, tm - 1:tm], cend_o[0])


def _proj(seq_len, x, g, w, mu, vecs, w2p, a2p, g2p, qg, kg, fb, vres, tm):
    n = x.shape[0]
    blocks = seq_len // tm
    full = lambda a: pl.BlockSpec(a.shape, lambda i: (0,) * a.ndim)
    tile = lambda c: pl.BlockSpec((tm, c), lambda i: (i, 0))
    in_specs = [tile(D_MODEL)] + [full(a) for a in (g, w, mu, vecs, w2p, a2p, g2p, qg, kg, fb)]
    args = [x, g, w, mu, vecs, w2p, a2p, g2p, qg, kg, fb]
    if vres is not None:
        v_first, v1p, v2p = vres
        in_specs += [tile(MIX_W), full(v1p), full(v2p)]
        args += [v_first, v1p, v2p]
    padded = N_HEADS * PAIR
    return pl.pallas_call(
        functools.partial(_proj_body, seq_len, vres is not None),
        grid=(n // tm,),
        in_specs=in_specs,
        out_specs=[tile(MIX_W)] * 8 + [tile(padded), tile(padded),
                                       pl.BlockSpec((1, padded, tm), lambda i: (i, 0, 0)),
                                       pl.BlockSpec((1, 8, blocks), lambda i: (i // blocks, 0, 0)),
                                       tile(MIX_W), tile(POOL_W)],
        out_shape=[jax.ShapeDtypeStruct((n, MIX_W), F32)] * 8
        + [jax.ShapeDtypeStruct((n, padded), BF16)] * 2
        + [jax.ShapeDtypeStruct((n // tm, padded, tm), BF16),
           jax.ShapeDtypeStruct((n // seq_len, 8, blocks), F32),
           jax.ShapeDtypeStruct((n, MIX_W), F32), jax.ShapeDtypeStruct((n, POOL_W), F32)],
        scratch_shapes=[pltpu.VMEM((8, RWKV_PAD), F32), pltpu.VMEM((1, FF_PAD), F32)],
        compiler_params=_params("arbitrary"),
        name="proj",
    )(*args)


def _fox_attn_body(q_ref, k_ref, vt_ref, cend_ref, qg_ref, kg_ref, o_ref):
    h = pl.program_id(1)
    tq = q_ref.shape[0]
    tk = vt_ref.shape[2]
    q_blocks = tq // tk
    diag = pl.program_id(2) * q_blocks
    nk = cend_ref.shape[2]
    q = q_ref[...]
    sub = lax.broadcasted_iota(jnp.int32, (8, 1), 0)
    blk = lax.broadcasted_iota(jnp.int32, (1, nk), 1)
    cend = jnp.sum(jnp.where(sub == h, cend_ref[0], 0.0), axis=0, keepdims=True)
    end_of = lambda b: jnp.sum(jnp.where(blk == b, cend, 0.0), axis=1, keepdims=True)
    c_ref = end_of(diag - 1)

    amax = lambda r: jnp.max(jnp.abs(r[...]), axis=1, keepdims=True)
    qk_bound = 8.0 * BF16_MARGIN * amax(qg_ref) * amax(kg_ref)
    needed = (blk < diag) & (c_ref - cend > -(2.0 * qk_bound + UNDERFLOW_CUT))
    first = diag - jnp.sum(jnp.where(needed, 1, 0))

    def block(j, width, carry, diagonal):
        m, acc = carry
        start = pl.multiple_of(j * tk, tk)
        z = lax.dot_general(k_ref[pl.ds(start, width * tk), :], q, (((1,), (1,)), ((), ())),
                            preferred_element_type=F32)
        if diagonal:
            key = lax.broadcasted_iota(jnp.int32, z.shape, 0)
            qry = lax.broadcasted_iota(jnp.int32, z.shape, 1)
            z = jnp.where(key <= qry, z, -1e30)
        for u in range(width):
            z_u = z[u * tk:(u + 1) * tk, :]
            s_u = (end_of(j + u - 1) - c_ref) * LOG2E
            m_new = jnp.maximum(m, jnp.max(z_u, axis=0, keepdims=True) - s_u)
            p_u = jnp.exp2(z_u - (m_new + s_u)).astype(BF16)
            acc = acc * jnp.exp2(m - m_new) + jnp.dot(vt_ref[j + u], p_u, preferred_element_type=F32)
            m = m_new
        return m, acc

    carry = (jnp.full((1, tq), -1e30, F32), jnp.zeros((PAIR, tq), F32))
    carry = block(diag, q_blocks, carry, True)
    i = diag
    wide = jnp.minimum((i - first + FOX_WIDE - 1) // FOX_WIDE, i // FOX_WIDE)
    lo = i - wide * FOX_WIDE
    carry = lax.fori_loop(0, wide, lambda g, c: block(lo + g * FOX_WIDE, FOX_WIDE, c, False), carry)
    carry = lax.fori_loop(first, lo, lambda j, c: block(j, 1, c, False), carry)
    acc = carry[1]
    value_row = lax.broadcasted_iota(jnp.int32, (PAIR, 1), 0) < HEAD_DIM
    out_t = jnp.where(value_row, acc * (1.0 / acc[HEAD_DIM:HEAD_DIM + 1, :]), 0.0)
    o_ref[...] = out_t.T


def _fox_attn(seq_len, q, k, vt, cend, qg, kg, tk, q_blocks):
    n = q.shape[0]
    tq = tk * q_blocks
    nq, nk = seq_len // tq, seq_len // tk
    full = lambda a: pl.BlockSpec(a.shape, lambda bi, h, i: (0,) * a.ndim)
    return pl.pallas_call(
        _fox_attn_body,
        grid=(n // seq_len, N_HEADS, nq),
        in_specs=[pl.BlockSpec((tq, PAIR), lambda bi, h, i: (bi * nq + i, h)),
                  pl.BlockSpec((seq_len, PAIR), lambda bi, h, i: (bi, h)),
                  pl.BlockSpec((nk, PAIR, tk), lambda bi, h, i: (bi, h, 0)),
                  pl.BlockSpec((1, 8, nk), lambda bi, h, i: (bi, 0, 0)),
                  full(qg), full(kg)],
        out_specs=pl.BlockSpec((tq, PAIR), lambda bi, h, i: (bi * nq + i, h)),
        out_shape=jax.ShapeDtypeStruct((n, N_HEADS * PAIR), F32),
        compiler_params=_params("parallel", "parallel", "arbitrary"),
        name="fox_attn",
    )(q, k, vt, cend, qg, kg)


def _mix_out_body(seq_len, x_ref, ys_ref, g_ref, bonus_ref, at_ref, fg_ref, pool_ref, halo_ref,
                  ln_ref, pw_ref, ps_ref, wo_ref, pg_ref, o_ref):
    tm = x_ref.shape[0]
    row0 = lax.rem(pl.program_id(0) * tm, seq_len)

    y = ys_ref[...]
    d = y - _head_sums(y) * (1.0 / HEAD_DIM)
    var = _head_sums(d * d) * (1.0 / HEAD_DIM)
    y_rwkv = (d * lax.rsqrt(var + GN_EPS) * ln_ref[0:1, :] + ln_ref[1:2, :] + bonus_ref[...]) * g_ref[...]

    left = lax.broadcasted_iota(jnp.int32, (1, PAIR), 1) < HEAD_DIM
    at = at_ref[...]
    packed = [jnp.where(left, at[:, 2 * p * PAIR:(2 * p + 1) * PAIR],
                        pltpu.roll(at[:, (2 * p + 1) * PAIR:(2 * p + 2) * PAIR], HEAD_DIM, axis=1))
              for p in range(N_PAIRS)]
    y_fox = jnp.concatenate(packed, axis=1) * _sigmoid(fg_ref[...])

    pin = pool_ref[...]
    halo = jnp.where(row0 == 0, 0.0, halo_ref[...])
    s = jnp.concatenate([halo, pin], axis=0)
    sums = []
    for shift in (1, 2, 4, 8):
        s = s + pltpu.roll(s, shift, axis=0)
        sums.append(s[POOL_HALO:, :])
    grp = lax.broadcasted_iota(jnp.int32, (1, POOL_W), 1) // HEAD_DIM
    win_sum = jnp.where(grp == 0, sums[0], jnp.where(grp == 1, sums[1], jnp.where(grp == 2, sums[2], sums[3])))
    win = jnp.where(grp == 0, 2.0, jnp.where(grp == 1, 4.0, jnp.where(grp == 2, 8.0, 16.0)))
    pos = (row0 + lax.broadcasted_iota(jnp.int32, (tm, 1), 0) + 1).astype(F32)
    u = win_sum / jnp.minimum(pos, win) - pin
    y_pool = _dot(u, pw_ref[...]) * ps_ref[...]

    mixed = (_dot(y_rwkv, wo_ref[0:384, :]) + _dot(y_fox, wo_ref[384:768, :])
             + _dot(y_pool, wo_ref[768:1024, :]))
    o_ref[...] = x_ref[...] + _rms(mixed, pg_ref[...])


def _mix_out(seq_len, x, y_scan, g, bonus, attn, fg, pool_in, ln, pw, ps, wo, pg, tm=512):
    n = x.shape[0]
    full = lambda a: pl.BlockSpec(a.shape, lambda i: (0,) * a.ndim)
    tile = pl.BlockSpec((tm, MIX_W), lambda i: (i, 0))
    return pl.pallas_call(
        functools.partial(_mix_out_body, seq_len),
        grid=(n // tm,),
        in_specs=[pl.BlockSpec((tm, D_MODEL), lambda i: (i, 0)), tile, tile, tile,
                  pl.BlockSpec((tm, N_HEADS * PAIR), lambda i: (i, 0)), tile,
                  pl.BlockSpec((tm, POOL_W), lambda i: (i, 0)),
                  pl.BlockSpec((POOL_HALO, POOL_W),
                               lambda i: (jnp.maximum(i * (tm // POOL_HALO) - 1, 0), 0)),
                  full(ln), full(pw), full(ps), full(wo), full(pg)],
        out_specs=pl.BlockSpec((tm, D_MODEL), lambda i: (i, 0)),
        out_shape=jax.ShapeDtypeStruct((n, D_MODEL), F32),
        compiler_params=_params("parallel"),
        name="mix_out",
    )(x, y_scan, g, bonus, attn, fg, pool_in, pool_in, ln, pw, ps, wo, pg)


def _ffn_body(x_ref, g1_ref, wgu_ref, wd_ref, g2_ref, o_ref):
    x = x_ref[...]
    h = _rms(x, g1_ref[...]).astype(BF16)
    acc = jnp.zeros(x.shape, F32)
    for c in range(FFN_HIDDEN // FFN_CHUNK):
        c0 = c * FFN_CHUNK
        gate = jnp.dot(h, wgu_ref[:, c0:c0 + FFN_CHUNK], preferred_element_type=F32)
        up = jnp.dot(h, wgu_ref[:, FFN_HIDDEN + c0:FFN_HIDDEN + c0 + FFN_CHUNK],
                     preferred_element_type=F32)
        act = (gate * _sigmoid(gate) * up).astype(BF16)
        acc = acc + jnp.dot(act, wd_ref[c0:c0 + FFN_CHUNK, :], preferred_element_type=F32)
    o_ref[...] = x + _rms(acc, g2_ref[...])


def _ffn(x, g1, wgu, wd, g2, tm=512):
    n = x.shape[0]
    full = lambda a: pl.BlockSpec(a.shape, lambda i: (0,) * a.ndim)
    tile = pl.BlockSpec((tm, D_MODEL), lambda i: (i, 0))
    return pl.pallas_call(
        _ffn_body,
        grid=(n // tm,),
        in_specs=[tile, full(g1), full(wgu), full(wd), full(g2)],
        out_specs=tile,
        out_shape=jax.ShapeDtypeStruct((n, D_MODEL), F32),
        compiler_params=_params("parallel"),
        name="ffn",
    )(x, g1, wgu, wd, g2)


def _pad_rows(a, rows, at=0):
    out = jnp.zeros((rows, a.shape[1]), a.dtype)
    return lax.dynamic_update_slice(out, a, (at, 0))


def kernel(x, mix_pre_g, mix_post_g, ffn_pre_g, ffn_post_g, w_in, w_out, rwkv_mu, rwkv_w0, rwkv_w2,
           rwkv_a0, rwkv_a2, rwkv_g2, rwkv_v0, rwkv_v1, rwkv_v2, rwkv_k_k, rwkv_k_a, rwkv_r_k,
           rwkv_ln_w, rwkv_ln_b, fox_q_g, fox_k_g, fox_f_b, pool_w, pool_scale, ffn_w_gu, ffn_w_down):
    batch, seq_len, d_model = x.shape
    assert d_model == D_MODEL and w_in.shape[2] == RWKV_IN + FOX_IN + N_HEADS + POOL_W
    assert seq_len % 512 == 0 and seq_len % (FOX_Q_BLOCKS * FOX_BLOCK) == 0
    n = batch * seq_len
    depth = w_in.shape[0]
    row = lambda a: a.reshape(1, -1).astype(F32)
    xf = x.reshape(n, D_MODEL)
    w_arr = _w_regroup(w_in)
    v_first = None
    for l in range(depth):
        mu = jnp.pad(rwkv_mu[l], (0, RWKV_PAD - RWKV_IN)).reshape(1, RWKV_PAD)
        v0 = rwkv_v0[l - 1] if l > 0 else jnp.zeros((MIX_W,), F32)
        vecs = jnp.stack([rwkv_w0[l], rwkv_a0[l], rwkv_k_k[l], rwkv_k_a[l], rwkv_r_k[l].reshape(-1), v0,
                          jnp.zeros((MIX_W,), F32), jnp.zeros((MIX_W,), F32)])
        w2p = _pad_rows(rwkv_w2[l], PAIR, 0).astype(BF16)
        a2p = _pad_rows(rwkv_a2[l], PAIR, DECAY_LORA).astype(BF16)
        g2p = _pad_rows(rwkv_g2[l], RWKV_PAD - 3 * MIX_W - PAIR, 0).astype(BF16)
        vres = None
        if l > 0:
            v1p = jnp.pad(rwkv_v1[l - 1], ((0, 0), (0, PAIR - VRES_LORA))).astype(BF16)
            v2p = _pad_rows(rwkv_v2[l - 1], PAIR, 0).astype(BF16)
            vres = (v_first, v1p, v2p)
        tile6 = lambda a: jnp.tile(a, N_HEADS).reshape(1, MIX_W)
        fb = jnp.pad(fox_f_b[l], (0, FF_PAD - N_HEADS)).reshape(1, FF_PAD)
        qg, kg = tile6(fox_q_g[l]), tile6(fox_k_g[l])
        (r, k, v, lw, kk, b, g, bonus, q_a, k_a, v_t, cend, fg, pool_in) = _proj(
            seq_len, xf, row(mix_pre_g[l]), w_arr[l], mu, vecs, w2p, a2p, g2p, qg, kg, fb, vres, FOX_BLOCK)
        if l == 0:
            v_first = v
        y_scan = _rwkv_scan(seq_len, r, lw, k, v, kk, b)
        attn = _fox_attn(seq_len, q_a, k_a, v_t, cend, qg, kg, FOX_BLOCK, FOX_Q_BLOCKS)

        ln = jnp.stack([rwkv_ln_w[l], rwkv_ln_b[l]] + [jnp.zeros((MIX_W,), F32)] * 6)
        pw = jax.scipy.linalg.block_diag(*[pool_w[l, gi] for gi in range(4)]).astype(BF16)
        xf = _mix_out(seq_len, xf, y_scan, g, bonus, attn, fg, pool_in, ln, pw,
                      row(pool_scale[l]), w_out[l].astype(BF16), row(mix_post_g[l]))
        xf = _ffn(xf, row(ffn_pre_g[l]), ffn_w_gu[l].astype(BF16), ffn_w_down[l].astype(BF16),
                  row(ffn_post_g[l]))
    return xf.reshape(batch, seq_len, D_MODEL)
```

```python
import functools

import jax
import jax.numpy as jnp
from jax import lax
from jax.experimental import pallas as pl
from jax.experimental.pallas import tpu as pltpu

F32 = jnp.float32
BF16 = jnp.bfloat16

D_MODEL = 1024
HEAD_DIM = 64
PAIR = 2 * HEAD_DIM
N_HEADS = 6
N_PAIRS = N_HEADS // 2
MIX_W = N_HEADS * HEAD_DIM
POOL_W = 256
POOL_HALO = 16
FFN_HIDDEN = 2816
FFN_CHUNK = 256
RMS_EPS = 1e-6
GN_EPS = HEAD_DIM * 1e-5

DECAY_LORA, ICLR_LORA, GATE_LORA, VRES_LORA = 64, 64, 160, 32
RWKV_IN = 3 * MIX_W + DECAY_LORA + ICLR_LORA + GATE_LORA
RWKV_PAD = 1536
FOX_IN = 4 * MIX_W
FF_PAD = 128
IN_WIDTHS = (RWKV_PAD, FOX_IN, FF_PAD, POOL_W)
IN_SPLITS = tuple((sum(IN_WIDTHS[:i]), sum(IN_WIDTHS[:i + 1])) for i in range(len(IN_WIDTHS)))
IN_TOTAL = sum(IN_WIDTHS)

LOG2E = 1.4426950408889634
FOX_BLOCK = 256
FOX_Q_BLOCKS = 4
FOX_WIDE = 8
UNDERFLOW_CUT = 110.0
BF16_MARGIN = 1.02
SCAN_CHUNK = 64
VMEM_LIMIT = 56 * 1024 * 1024


def _params(*sem):
    return pltpu.CompilerParams(dimension_semantics=sem, vmem_limit_bytes=VMEM_LIMIT)


def _dot(a, b):
    return jnp.dot(a.astype(BF16), b.astype(BF16), preferred_element_type=F32)


def _dot_nt(a, b):
    return lax.dot_general(a.astype(BF16), b.astype(BF16), (((1,), (1,)), ((), ())),
                           preferred_element_type=F32)


def _split3(x):
    hi = x.astype(BF16)
    r1 = x - hi.astype(F32)
    mid = r1.astype(BF16)
    lo = (r1 - mid.astype(F32)).astype(BF16)
    return hi, mid, lo


def _head_sums(x):
    left = lax.broadcasted_iota(jnp.int32, (1, PAIR), 1) < HEAD_DIM
    outs = []
    for p in range(x.shape[1] // PAIR):
        xp = x[:, p * PAIR:(p + 1) * PAIR]
        lsum = jnp.sum(jnp.where(left, xp, 0.0), axis=1, keepdims=True)
        rsum = jnp.sum(jnp.where(left, 0.0, xp), axis=1, keepdims=True)
        outs.append(jnp.where(left, lsum, rsum))
    return jnp.concatenate(outs, axis=1)


def _dot3r(w, x):
    hi, mid, lo = _split3(x)
    d = lambda t: jnp.dot(w, t, preferred_element_type=F32)
    return d(hi) + d(mid) + d(lo)


def _rms(x, g):
    ms = jnp.mean(x * x, axis=-1, keepdims=True)
    return x * lax.rsqrt(ms + RMS_EPS) * g


def _softplus(z):
    return jnp.maximum(z, 0.0) + jnp.log(1.0 + jnp.exp(-jnp.abs(z)))


def _sigmoid(z):
    return 1.0 / (1.0 + jnp.exp(-z))


def _tri(n, inclusive, dtype):
    r = lax.broadcasted_iota(jnp.int32, (n, n), 0)
    c = lax.broadcasted_iota(jnp.int32, (n, n), 1)
    keep = (r >= c) if inclusive else (r > c)
    return jnp.where(keep, 1.0, 0.0).astype(dtype)


def _w_regroup_body(w_ref, o_ref):
    w = w_ref[0]
    zeros = lambda c: jnp.zeros((w.shape[0], c), F32)
    fox0, ff0, pool0 = RWKV_IN, RWKV_IN + FOX_IN, RWKV_IN + FOX_IN + N_HEADS
    o_ref[0] = jnp.concatenate(
        [w[:, :fox0], zeros(RWKV_PAD - RWKV_IN), w[:, fox0:ff0], w[:, ff0:pool0],
         zeros(FF_PAD - N_HEADS), w[:, pool0:]], axis=1).astype(BF16)


def _w_regroup(w_in, rows=128):
    depth, d, cols = w_in.shape
    return pl.pallas_call(
        _w_regroup_body,
        grid=(depth, d // rows),
        in_specs=[pl.BlockSpec((1, rows, cols), lambda l, i: (l, i, 0))],
        out_specs=pl.BlockSpec((1, rows, IN_TOTAL), lambda l, i: (l, i, 0)),
        out_shape=jax.ShapeDtypeStruct((depth, d, IN_TOTAL), BF16),
        compiler_params=_params("parallel", "parallel"),
        name="w_regroup",
    )(w_in)


def _rwkv_features(f, halo, mu, vecs, w2, a2, g2, vres):
    tm = f.shape[0]
    row = lax.broadcasted_iota(jnp.int32, (tm, 1), 0)
    prev = jnp.where(row == 0, halo, pltpu.roll(f, 1, axis=0))
    f = f + (prev - f) * mu
    r, k, v = f[:, 0:MIX_W], f[:, MIX_W:2 * MIX_W], f[:, 2 * MIX_W:3 * MIX_W]
    lora0 = 3 * MIX_W
    wa = f[:, lora0:lora0 + PAIR]
    gd = f[:, lora0 + PAIR:RWKV_PAD]
    w0, a0, k_k, k_a, r_k, v0 = (vecs[j:j + 1, :] for j in range(6))

    w_log = -_softplus(-(w0 + _dot(jnp.tanh(wa), w2))) - 0.5
    lw = -jnp.exp(w_log)
    a = _sigmoid(a0 + _dot(wa, a2))
    g = _dot(_sigmoid(gd), g2)
    if vres is not None:
        v_first, v1, v2 = vres
        v = v + (v_first - v) * _sigmoid(v0 + _dot(_dot(v, v1), v2))
    kk = k * k_k
    kk = kk * lax.rsqrt(jnp.maximum(_head_sums(kk * kk), 1e-24))
    k = k * (1.0 + (a - 1.0) * k_a)
    return r, k, v, lw, kk, kk * a, g, _head_sums(r * k * r_k) * v


def _scan_body(chunks_per_step, r_ref, lw_ref, k_ref, v_ref, kk_ref, b_ref, y_ref, h_ref):
    c_len = SCAN_CHUNK
    n_batch = r_ref.shape[0]

    @pl.when(pl.program_id(0) == 0)
    def _():
        h_ref[...] = jnp.zeros_like(h_ref)

    lane = lax.broadcasted_iota(jnp.int32, (1, PAIR), 1)
    left = lane < HEAD_DIM
    ri = lax.broadcasted_iota(jnp.int32, (c_len, PAIR), 0)
    ci = lax.broadcasted_iota(jnp.int32, (c_len, PAIR), 1) & (HEAD_DIM - 1)
    strict, incl, diag = ri > ci, ri >= ci, ri == ci
    tri = _tri(c_len, True, BF16)
    eye = jnp.where(diag, 1.0, 0.0)

    def stack(x):
        return jnp.concatenate([jnp.where(left, x, 0.0), jnp.where(left, 0.0, x)], axis=0)

    def side_t(x):
        z = stack(x).T
        return z[:c_len] + z[c_len:]

    seqs = [(bi, p) for bi in range(n_batch) for p in range(N_PAIRS)]
    items = [(bi, p, c) for (bi, p) in seqs for c in range(chunks_per_step)]
    window = lambda bi, p, c: (bi, slice(c * c_len, (c + 1) * c_len), slice(p * PAIR, (p + 1) * PAIR))
    load = lambda ref: [ref[window(*it)] for it in items]
    each = lambda f, *cols: [f(*xs) for xs in zip(*cols)]

    lw = load(lw_ref)
    cum = each(lambda x: _dot3r(tri, x), lw)
    cum_end = each(lambda x: x[c_len - 1:c_len, :], cum)
    e_neg = each(lambda x: jnp.exp(-x), cum)
    e_tail = each(lambda x, xe: jnp.exp(xe - x), cum, cum_end)
    kk, b, k = load(kk_ref), load(b_ref), load(k_ref)
    l_a = each(lambda x, c_, w_: -x * jnp.exp(c_ - w_), kk, cum, lw)
    l_r = each(lambda x, c_: x * jnp.exp(c_), load(r_ref), cum)
    r_b = each(lambda x, e: stack(x * e), b, e_neg)
    r_k = each(lambda x, e: stack(x * e), k, e_neg)
    rb_t = each(lambda x, e: side_t(x * e), b, e_tail)
    rk_t = each(lambda x, e: side_t(x * e), k, e_tail)
    v_s = each(stack, load(v_ref))

    cat = lambda *xs: jnp.concatenate(xs, axis=0)
    part = lambda x, i: x[i * c_len:(i + 1) * c_len]
    l_ar = each(cat, l_a, l_r)
    a_b = each(_dot_nt, l_ar, r_b)
    a_k = each(_dot_nt, l_ar, r_k)
    a_ab = each(lambda a: jnp.where(strict, part(a, 0), 0.0), a_b)
    a_rb = each(lambda a: jnp.where(incl, part(a, 1), 0.0), a_b)
    a_ak = each(lambda a: jnp.where(strict, part(a, 0), 0.0), a_k)
    a_rk = each(lambda a: jnp.where(incl, part(a, 1), 0.0), a_k)

    t_inv = each(lambda a: eye + a, a_ab)
    x = each(lambda a: _dot(a, stack(a)), a_ab)
    for _ in range(4):
        xt = each(lambda x_, t: _dot(cat(x_, t), stack(x_)), x, t_inv)
        x = each(lambda p_: part(p_, 0), xt)
        t_inv = each(lambda t, p_: t + part(p_, 1), t_inv, xt)
    t_inv = each(lambda t, x_: t + _dot(t, stack(x_)), t_inv, x)

    t_a = each(lambda t, l: stack(_dot(t, stack(l))), t_inv, l_a)
    av = each(lambda a, a2, kt, v: _dot(cat(a, a2, kt), v), a_ak, a_rk, rk_t, v_s)
    w_ = each(lambda t, p_: stack(_dot(t, stack(part(p_, 0)))), t_inv, av)
    on_ta = each(lambda a, bt, t: _dot(cat(a, bt), t), a_rb, rb_t, t_a)
    on_w = each(lambda a, bt, w: _dot(cat(a, bt), w), a_rb, rb_t, w_)
    r_q = each(lambda l, p_: l + part(p_, 0), l_r, on_ta)
    m_ = each(lambda ce, p_: jnp.where(diag, jnp.exp(ce), 0.0) + part(p_, 1), cum_end, on_ta)
    y_0 = each(lambda p_, q_: part(p_, 0) + part(q_, 1), on_w, av)
    n_ = each(lambda p_, q_: part(p_, 1) + part(q_, 2), on_w, av)

    for s, (bi, p) in enumerate(seqs):
        h = h_ref[s]
        for c in range(chunks_per_step):
            idx = s * chunks_per_step + c
            on_h = _dot(cat(r_q[idx], m_[idx]), stack(h))
            y_ref[window(bi, p, c)] = part(on_h, 0) + y_0[idx]
            h = part(on_h, 1) + n_[idx]
        h_ref[s] = h


def _rwkv_scan(seq_len, r, lw, k, v, kk, b, chunks_per_step=2):
    n = r.shape[0]
    n_batch = n // seq_len
    rows = chunks_per_step * SCAN_CHUNK
    spec = pl.BlockSpec((n_batch, rows, MIX_W), lambda j: (0, j, 0))
    args = [a.reshape(n_batch, seq_len, MIX_W) for a in (r, lw, k, v, kk, b)]
    y = pl.pallas_call(
        functools.partial(_scan_body, chunks_per_step),
        grid=(seq_len // rows,),
        in_specs=[spec] * 6,
        out_specs=spec,
        out_shape=jax.ShapeDtypeStruct((n_batch, seq_len, MIX_W), F32),
        scratch_shapes=[pltpu.VMEM((n_batch * N_PAIRS, HEAD_DIM, PAIR), F32)],
        compiler_params=_params("arbitrary"),
        name="rwkv_scan",
    )(*args)
    return y.reshape(n, MIX_W)


def _fox_features(q, k, v, ff, qg, kg, fb):
    tm = q.shape[0]

    def qk_norm(x, g):
        ms = _head_sums(x * x) * (1.0 / HEAD_DIM)
        return x * lax.rsqrt(ms + RMS_EPS) * g

    lane = lax.broadcasted_iota(jnp.int32, (1, PAIR), 1)
    left = lane < HEAD_DIM

    def pad_heads(x, aug):
        cols = []
        for h in range(N_HEADS):
            src = x[:, (h // 2) * PAIR:(h // 2 + 1) * PAIR]
            if h % 2:
                src = pltpu.roll(src, HEAD_DIM, axis=1)
            cols.append(jnp.where(left, src, aug(h)))
        return jnp.concatenate(cols, axis=1).astype(BF16)

    log_f = -_softplus(-(ff + fb))
    local = _dot3r(_tri(tm, True, BF16), log_f)
    bias = local * (-LOG2E)

    def k_aug(h):
        col = jnp.sum(jnp.where(lane == h, bias, 0.0), axis=1, keepdims=True)
        hi, mid, lo = (p.astype(F32) for p in _split3(col))
        return jnp.where(lane == HEAD_DIM, hi,
                         jnp.where(lane == HEAD_DIM + 1, mid, jnp.where(lane == HEAD_DIM + 2, lo, 0.0)))

    q_aug = jnp.where((lane >= HEAD_DIM) & (lane < HEAD_DIM + 3), 1.0, 0.0)
    q_a = pad_heads(qk_norm(q, qg) * (LOG2E * HEAD_DIM ** -0.5), lambda h: q_aug)
    k_a = pad_heads(qk_norm(k, kg), k_aug)

    v_t = v.T
    ones_row = jnp.where(lax.broadcasted_iota(jnp.int32, (HEAD_DIM, tm), 0) == 0, 1.0, 0.0)
    rows = []
    for h in range(N_HEADS):
        rows += [v_t[h * HEAD_DIM:(h + 1) * HEAD_DIM, :], ones_row]
    return q_a, k_a, jnp.concatenate(rows, axis=0).astype(BF16), local


def _proj_body(seq_len, has_vres, *refs):
    n_in = 14 if has_vres else 11
    x_ref, g_ref, w_ref, mu_ref, vec_ref, w2_ref, a2_ref, g2_ref, qg_ref, kg_ref, fb_ref = refs[:11]
    vres_refs = refs[11:14] if has_vres else None
    (r_o, k_o, v_o, lw_o, kk_o, b_o, g_o, bonus_o,
     qa_o, ka_o, vt_o, cend_o, fg_o, pool_o) = refs[n_in:n_in + 14]
    prev_ref, carry_ref = refs[n_in + 14:]
    tm = x_ref.shape[0]
    blocks = seq_len // tm
    j = lax.rem(pl.program_id(0), blocks)

    @pl.when(j == 0)
    def _():
        prev_ref[...] = jnp.zeros_like(prev_ref)
        carry_ref[...] = jnp.zeros_like(carry_ref)
        cend_o[...] = jnp.zeros_like(cend_o)

    h = _rms(x_ref[...], g_ref[...]).astype(BF16)
    feats, fox, ff, pool = (jnp.dot(h, w_ref[:, c0:c1], preferred_element_type=F32) for c0, c1 in IN_SPLITS)
    pool_o[...] = pool
    fg_o[...] = fox[:, 3 * MIX_W:]

    halo = prev_ref[7:8, :]
    prev_ref[...] = feats[tm - 8:, :]
    vres = None if vres_refs is None else tuple(ref[...] for ref in vres_refs)
    outs = _rwkv_features(feats, halo, mu_ref[...], vec_ref[...], w2_ref[...], a2_ref[...], g2_ref[...], vres)
    for o_ref, val in zip((r_o, k_o, v_o, lw_o, kk_o, b_o, g_o, bonus_o), outs):
        o_ref[...] = val

    q_a, k_a, v_t, local = _fox_features(fox[:, :MIX_W], fox[:, MIX_W:2 * MIX_W], fox[:, 2 * MIX_W:3 * MIX_W],
                                         ff, qg_ref[...], kg_ref[...], fb_ref[...])
    qa_o[...] = q_a
    ka_o[...] = k_a
    vt_o[0] = v_t
    cum = local + carry_ref[...]
    carry_ref[...] = cum[tm - 1:tm, :]
    blk = lax.broadcasted_iota(jnp.int32, (1, blocks), 1)
    cend_o[0] = jnp.where(blk == j, cum.T[0:8, tm - 1:tm], cend_o[0])


def _proj(seq_len, x, g, w, mu, vecs, w2p, a2p, g2p, qg, kg, fb, vres, tm):
    n = x.shape[0]
    blocks = seq_len // tm
    full = lambda a: pl.BlockSpec(a.shape, lambda i: (0,) * a.ndim)
    tile = lambda c: pl.BlockSpec((tm, c), lambda i: (i, 0))
    in_specs = [tile(D_MODEL)] + [full(a) for a in (g, w, mu, vecs, w2p, a2p, g2p, qg, kg, fb)]
    args = [x, g, w, mu, vecs, w2p, a2p, g2p, qg, kg, fb]
    if vres is not None:
        v_first, v1p, v2p = vres
        in_specs += [tile(MIX_W), full(v1p), full(v2p)]
        args += [v_first, v1p, v2p]
    padded = N_HEADS * PAIR
    return pl.pallas_call(
        functools.partial(_proj_body, seq_len, vres is not None),
        grid=(n // tm,),
        in_specs=in_specs,
        out_specs=[tile(MIX_W)] * 8 + [tile(padded), tile(padded),
                                       pl.BlockSpec((1, padded, tm), lambda i: (i, 0, 0)),
                                       pl.BlockSpec((1, 8, blocks), lambda i: (i // blocks, 0, 0)),
                                       tile(MIX_W), tile(POOL_W)],
        out_shape=[jax.ShapeDtypeStruct((n, MIX_W), F32)] * 8
        + [jax.ShapeDtypeStruct((n, padded), BF16)] * 2
        + [jax.ShapeDtypeStruct((n // tm, padded, tm), BF16),
           jax.ShapeDtypeStruct((n // seq_len, 8, blocks), F32),
           jax.ShapeDtypeStruct((n, MIX_W), F32), jax.ShapeDtypeStruct((n, POOL_W), F32)],
        scratch_shapes=[pltpu.VMEM((8, RWKV_PAD), F32), pltpu.VMEM((1, FF_PAD), F32)],
        compiler_params=_params("arbitrary"),
        name="proj",
    )(*args)


def _fox_attn_body(q_ref, k_ref, vt_ref, cend_ref, qg_ref, kg_ref, o_ref):
    h = pl.program_id(1)
    tq = q_ref.shape[0]
    tk = vt_ref.shape[2]
    q_blocks = tq // tk
    diag = pl.program_id(2) * q_blocks
    nk = cend_ref.shape[2]
    q = q_ref[...]
    sub = lax.broadcasted_iota(jnp.int32, (8, 1), 0)
    blk = lax.broadcasted_iota(jnp.int32, (1, nk), 1)
    cend = jnp.sum(jnp.where(sub == h, cend_ref[0], 0.0), axis=0, keepdims=True)
    end_of = lambda b: jnp.sum(jnp.where(blk == b, cend, 0.0), axis=1, keepdims=True)
    c_ref = end_of(diag - 1)

    amax = lambda r: jnp.max(jnp.abs(r[...]), axis=1, keepdims=True)
    qk_bound = 8.0 * BF16_MARGIN * amax(qg_ref) * amax(kg_ref)
    needed = (blk < diag) & (c_ref - cend > -(2.0 * qk_bound + UNDERFLOW_CUT))
    first = diag - jnp.sum(jnp.where(needed, 1, 0))

    def block(j, width, carry, diagonal):
        m, acc = carry
        start = pl.multiple_of(j * tk, tk)
        z = lax.dot_general(k_ref[pl.ds(start, width * tk), :], q, (((1,), (1,)), ((), ())),
                            preferred_element_type=F32)
        if diagonal:
            key = lax.broadcasted_iota(jnp.int32, z.shape, 0)
            qry = lax.broadcasted_iota(jnp.int32, z.shape, 1)
            z = jnp.where(key <= qry, z, -1e30)
        for u in range(width):
            z_u = z[u * tk:(u + 1) * tk, :]
            s_u = (end_of(j + u - 1) - c_ref) * LOG2E
            m_new = jnp.maximum(m, jnp.max(z_u, axis=0, keepdims=True) - s_u)
            p_u = jnp.exp2(z_u - (m_new + s_u)).astype(BF16)
            acc = acc * jnp.exp2(m - m_new) + jnp.dot(vt_ref[j + u], p_u, preferred_element_type=F32)
            m = m_new
        return m, acc

    carry = (jnp.full((1, tq), -1e30, F32), jnp.zeros((PAIR, tq), F32))
    carry = block(diag, q_blocks, carry, True)
    i = diag
    wide = jnp.minimum((i - first + FOX_WIDE - 1) // FOX_WIDE, i // FOX_WIDE)
    lo = i - wide * FOX_WIDE
    carry = lax.fori_loop(0, wide, lambda g, c: block(lo + g * FOX_WIDE, FOX_WIDE, c, False), carry)
    carry = lax.fori_loop(first, lo, lambda j, c: block(j, 1, c, False), carry)
    acc = carry[1]
    value_row = lax.broadcasted_iota(jnp.int32, (PAIR, 1), 0) < HEAD_DIM
    out_t = jnp.where(value_row, acc * (1.0 / acc[HEAD_DIM:HEAD_DIM + 1, :]), 0.0)
    o_ref[...] = out_t.T


def _fox_attn(seq_len, q, k, vt, cend, qg, kg, tk, q_blocks):
    n = q.shape[0]
    tq = tk * q_blocks
    nq, nk = seq_len // tq, seq_len // tk
    full = lambda a: pl.BlockSpec(a.shape, lambda bi, h, i: (0,) * a.ndim)
    return pl.pallas_call(
        _fox_attn_body,
        grid=(n // seq_len, N_HEADS, nq),
        in_specs=[pl.BlockSpec((tq, PAIR), lambda bi, h, i: (bi * nq + i, h)),
                  pl.BlockSpec((seq_len, PAIR), lambda bi, h, i: (bi, h)),
                  pl.BlockSpec((nk, PAIR, tk), lambda bi, h, i: (bi, h, 0)),
                  pl.BlockSpec((1, 8, nk), lambda bi, h, i: (bi, 0, 0)),
                  full(qg), full(kg)],
        out_specs=pl.BlockSpec((tq, PAIR), lambda bi, h, i: (bi * nq + i, h)),
        out_shape=jax.ShapeDtypeStruct((n, N_HEADS * PAIR), F32),
        compiler_params=_params("parallel", "parallel", "arbitrary"),
        name="fox_attn",
    )(q, k, vt, cend, qg, kg)


def _mix_out_body(seq_len, x_ref, ys_ref, g_ref, bonus_ref, at_ref, fg_ref, pool_ref, halo_ref,
                  ln_ref, pw_ref, ps_ref, wo_ref, pg_ref, o_ref):
    tm = x_ref.shape[0]
    row0 = lax.rem(pl.program_id(0) * tm, seq_len)

    y = ys_ref[...]
    d = y - _head_sums(y) * (1.0 / HEAD_DIM)
    var = _head_sums(d * d) * (1.0 / HEAD_DIM)
    y_rwkv = (d * lax.rsqrt(var + GN_EPS) * ln_ref[0:1, :] + ln_ref[1:2, :] + bonus_ref[...]) * g_ref[...]

    left = lax.broadcasted_iota(jnp.int32, (1, PAIR), 1) < HEAD_DIM
    at = at_ref[...]
    packed = [jnp.where(left, at[:, 2 * p * PAIR:(2 * p + 1) * PAIR],
                        pltpu.roll(at[:, (2 * p + 1) * PAIR:(2 * p + 2) * PAIR], HEAD_DIM, axis=1))
              for p in range(N_PAIRS)]
    y_fox = jnp.concatenate(packed, axis=1) * _sigmoid(fg_ref[...])

    pin = pool_ref[...]
    halo = jnp.where(row0 == 0, 0.0, halo_ref[...])
    s = jnp.concatenate([halo, pin], axis=0)
    sums = []
    for shift in (1, 2, 4, 8):
        s = s + pltpu.roll(s, shift, axis=0)
        sums.append(s[POOL_HALO:, :])
    grp = lax.broadcasted_iota(jnp.int32, (1, POOL_W), 1) // HEAD_DIM
    win_sum = jnp.where(grp == 0, sums[0], jnp.where(grp == 1, sums[1], jnp.where(grp == 2, sums[2], sums[3])))
    win = jnp.where(grp == 0, 2.0, jnp.where(grp == 1, 4.0, jnp.where(grp == 2, 8.0, 16.0)))
    pos = (row0 + lax.broadcasted_iota(jnp.int32, (tm, 1), 0) + 1).astype(F32)
    u = win_sum / jnp.minimum(pos, win) - pin
    y_pool = _dot(u, pw_ref[...]) * ps_ref[...]

    mixed = (_dot(y_rwkv, wo_ref[0:384, :]) + _dot(y_fox, wo_ref[384:768, :])
             + _dot(y_pool, wo_ref[768:1024, :]))
    o_ref[...] = x_ref[...] + _rms(mixed, pg_ref[...])


def _mix_out(seq_len, x, y_scan, g, bonus, attn, fg, pool_in, ln, pw, ps, wo, pg, tm=512):
    n = x.shape[0]
    full = lambda a: pl.BlockSpec(a.shape, lambda i: (0,) * a.ndim)
    tile = pl.BlockSpec((tm, MIX_W), lambda i: (i, 0))
    return pl.pallas_call(
        functools.partial(_mix_out_body, seq_len),
        grid=(n // tm,),
        in_specs=[pl.BlockSpec((tm, D_MODEL), lambda i: (i, 0)), tile, tile, tile,
                  pl.BlockSpec((tm, N_HEADS * PAIR), lambda i: (i, 0)), tile,
                  pl.BlockSpec((tm, POOL_W), lambda i: (i, 0)),
                  pl.BlockSpec((POOL_HALO, POOL_W),
                               lambda i: (jnp.maximum(i * (tm // POOL_HALO) - 1, 0), 0)),
                  full(ln), full(pw), full(ps), full(wo), full(pg)],
        out_specs=pl.BlockSpec((tm, D_MODEL), lambda i: (i, 0)),
        out_shape=jax.ShapeDtypeStruct((n, D_MODEL), F32),
        compiler_params=_params("parallel"),
        name="mix_out",
    )(x, y_scan, g, bonus, attn, fg, pool_in, pool_in, ln, pw, ps, wo, pg)


def _ffn_body(x_ref, g1_ref, wgu_ref, wd_ref, g2_ref, o_ref):
    x = x_ref[...]
    h = _rms(x, g1_ref[...]).astype(BF16)
    acc = jnp.zeros(x.shape, F32)
    for c in range(FFN_HIDDEN // FFN_CHUNK):
        c0 = c * FFN_CHUNK
        gate = jnp.dot(h, wgu_ref[:, c0:c0 + FFN_CHUNK], preferred_element_type=F32)
        up = jnp.dot(h, wgu_ref[:, FFN_HIDDEN + c0:FFN_HIDDEN + c0 + FFN_CHUNK],
                     preferred_element_type=F32)
        act = (gate * _sigmoid(gate) * up).astype(BF16)
        acc = acc + jnp.dot(act, wd_ref[c0:c0 + FFN_CHUNK, :], preferred_element_type=F32)
    o_ref[...] = x + _rms(acc, g2_ref[...])


def _ffn(x, g1, wgu, wd, g2, tm=512):
    n = x.shape[0]
    full = lambda a: pl.BlockSpec(a.shape, lambda i: (0,) * a.ndim)
    tile = pl.BlockSpec((tm, D_MODEL), lambda i: (i, 0))
    return pl.pallas_call(
        _ffn_body,
        grid=(n // tm,),
        in_specs=[tile, full(g1), full(wgu), full(wd), full(g2)],
        out_specs=tile,
        out_shape=jax.ShapeDtypeStruct((n, D_MODEL), F32),
        compiler_params=_params("parallel"),
        name="ffn",
    )(x, g1, wgu, wd, g2)


def _pad_rows(a, rows, at=0):
    out = jnp.zeros((rows, a.shape[1]), a.dtype)
    return lax.dynamic_update_slice(out, a, (at, 0))


def kernel(x, mix_pre_g, mix_post_g, ffn_pre_g, ffn_post_g, w_in, w_out, rwkv_mu, rwkv_w0, rwkv_w2,
           rwkv_a0, rwkv_a2, rwkv_g2, rwkv_v0, rwkv_v1, rwkv_v2, rwkv_k_k, rwkv_k_a, rwkv_r_k,
           rwkv_ln_w, rwkv_ln_b, fox_q_g, fox_k_g, fox_f_b, pool_w, pool_scale, ffn_w_gu, ffn_w_down):
    batch, seq_len, d_model = x.shape
    assert d_model == D_MODEL and w_in.shape[2] == RWKV_IN + FOX_IN + N_HEADS + POOL_W
    assert seq_len % 512 == 0 and seq_len % (FOX_Q_BLOCKS * FOX_BLOCK) == 0
    n = batch * seq_len
    depth = w_in.shape[0]
    row = lambda a: a.reshape(1, -1).astype(F32)
    xf = x.reshape(n, D_MODEL)
    w_arr = _w_regroup(w_in)
    v_first = None
    for l in range(depth):
        mu = jnp.pad(rwkv_mu[l], (0, RWKV_PAD - RWKV_IN)).reshape(1, RWKV_PAD)
        v0 = rwkv_v0[l - 1] if l > 0 else jnp.zeros((MIX_W,), F32)
        vecs = jnp.stack([rwkv_w0[l], rwkv_a0[l], rwkv_k_k[l], rwkv_k_a[l], rwkv_r_k[l].reshape(-1), v0,
                          jnp.zeros((MIX_W,), F32), jnp.zeros((MIX_W,), F32)])
        w2p = _pad_rows(rwkv_w2[l], PAIR, 0).astype(BF16)
        a2p = _pad_rows(rwkv_a2[l], PAIR, DECAY_LORA).astype(BF16)
        g2p = _pad_rows(rwkv_g2[l], RWKV_PAD - 3 * MIX_W - PAIR, 0).astype(BF16)
        vres = None
        if l > 0:
            v1p = jnp.pad(rwkv_v1[l - 1], ((0, 0), (0, PAIR - VRES_LORA))).astype(BF16)
            v2p = _pad_rows(rwkv_v2[l - 1], PAIR, 0).astype(BF16)
            vres = (v_first, v1p, v2p)
        tile6 = lambda a: jnp.tile(a, N_HEADS).reshape(1, MIX_W)
        fb = jnp.pad(fox_f_b[l], (0, FF_PAD - N_HEADS)).reshape(1, FF_PAD)
        qg, kg = tile6(fox_q_g[l]), tile6(fox_k_g[l])
        (r, k, v, lw, kk, b, g, bonus, q_a, k_a, v_t, cend, fg, pool_in) = _proj(
            seq_len, xf, row(mix_pre_g[l]), w_arr[l], mu, vecs, w2p, a2p, g2p, qg, kg, fb, vres, FOX_BLOCK)
        if l == 0:
            v_first = v
        y_scan = _rwkv_scan(seq_len, r, lw, k, v, kk, b)
        attn = _fox_attn(seq_len, q_a, k_a, v_t, cend, qg, kg, FOX_BLOCK, FOX_Q_BLOCKS)

        ln = jnp.stack([rwkv_ln_w[l], rwkv_ln_b[l]] + [jnp.zeros((MIX_W,), F32)] * 6)
        pw = jax.scipy.linalg.block_diag(*[pool_w[l, gi] for gi in range(4)]).astype(BF16)
        xf = _mix_out(seq_len, xf, y_scan, g, bonus, attn, fg, pool_in, ln, pw,
                      row(pool_scale[l]), w_out[l].astype(BF16), row(mix_post_g[l]))
        xf = _ffn(xf, row(ffn_pre_g[l]), ffn_w_gu[l].astype(BF16), ffn_w_down[l].astype(BF16),
                  row(ffn_post_g[l]))
    return xf.reshape(batch, seq_len, D_MODEL)
```
